```python
import jax, jax.numpy as jnp
from jax import lax
import numpy as np

D_MODEL = 1024
BATCH = 16
SEQ = 2048
DEPTH = 1

ATTN_HEADS = 4
NOPE_DIM = 128
ROPE_DIM = 64
V_DIM = 128
QK_DIM = NOPE_DIM + ROPE_DIM
ATTN_WIDTH = ATTN_HEADS * V_DIM
Q_LORA = 384
KV_LORA = 256
POOL_GROUPS = 4
POOL_WINDOWS = (2, 4, 8, 16)
POOL_WIDTH = D_MODEL - ATTN_WIDTH
POOL_CH = POOL_WIDTH // POOL_GROUPS
MIX_WIDTH = ATTN_WIDTH + POOL_WIDTH
IN_WIDTH = Q_LORA + KV_LORA + ROPE_DIM + POOL_WIDTH
D_FF = 4 * D_MODEL
Q_BLOCK = 128
ROPE_THETA = 10000.0
EPS = 1e-6

kernel_name = "hymba_mla_multiscale_pool_block"


def rmsnorm(x, g):
    xf = x.astype(jnp.float32)
    y = xf * lax.rsqrt(jnp.mean(xf * xf, axis=-1, keepdims=True) + EPS)
    return (y * g.astype(jnp.float32)).astype(x.dtype)


def rope(x, cos, sin):
    half = x.shape[-1] // 2
    x1, x2 = x[..., :half], x[..., half:]
    xf1, xf2 = x1.astype(jnp.float32), x2.astype(jnp.float32)
    out = jnp.concatenate([xf1 * cos - xf2 * sin, xf2 * cos + xf1 * sin], axis=-1)
    return out.astype(x.dtype)


def mla(c_q, c_kv, k_r, q_norm_g, w_q_up, kv_norm_g, w_kv_up, positions):
    B, S, _ = c_q.shape
    q = (rmsnorm(c_q, q_norm_g) @ w_q_up).reshape(B, S, ATTN_HEADS, QK_DIM)
    kv = (rmsnorm(c_kv, kv_norm_g) @ w_kv_up).reshape(B, S, ATTN_HEADS, NOPE_DIM + V_DIM)
    q_nope, q_pe = q[..., :NOPE_DIM], q[..., NOPE_DIM:]
    k_nope, v = kv[..., :NOPE_DIM], kv[..., NOPE_DIM:]

    inv_freq = 1.0 / (ROPE_THETA ** (jnp.arange(0, ROPE_DIM, 2, dtype=jnp.float32) / ROPE_DIM))
    ang = positions.astype(jnp.float32)[..., None] * inv_freq
    cos, sin = jnp.cos(ang), jnp.sin(ang)
    q_pe = rope(q_pe, cos[:, :, None, :], sin[:, :, None, :])
    k_pe = rope(k_r, cos, sin)
    k_pe = jnp.broadcast_to(k_pe[:, :, None, :], (B, S, ATTN_HEADS, ROPE_DIM))

    q = jnp.concatenate([q_nope, q_pe], axis=-1).transpose(0, 2, 1, 3)
    k = jnp.concatenate([k_nope, k_pe], axis=-1).transpose(0, 2, 1, 3)
    v = v.transpose(0, 2, 1, 3)
    scale = QK_DIM ** -0.5
    nb = S // Q_BLOCK
    qb = q.reshape(B, ATTN_HEADS, nb, Q_BLOCK, QK_DIM).transpose(2, 0, 1, 3, 4)
    kpos = jnp.arange(S)

    def attend(args):
        qblk, i = args
        s = jnp.einsum('bhqd,bhkd->bhqk', qblk, k).astype(jnp.float32) * scale
        qpos = i * Q_BLOCK + jnp.arange(Q_BLOCK)
        mask = qpos[:, None] >= kpos[None, :]
        p = jax.nn.softmax(jnp.where(mask, s, -jnp.inf), axis=-1)
        return jnp.einsum('bhqk,bhkd->bhqd', p.astype(v.dtype), v)

    o = lax.map(attend, (qb, jnp.arange(nb)))
    return o.transpose(1, 0, 3, 2, 4).reshape(B, S, ATTN_WIDTH)


def multiscale_pool(u, pool_w, pool_scale):
    B, S, _ = u.shape
    ug = u.reshape(B, S, POOL_GROUPS, POOL_CH)
    t = jnp.arange(S)
    outs = []
    for g, w in enumerate(POOL_WINDOWS):
        xg = ug[:, :, g, :].astype(jnp.float32)
        cs = jnp.cumsum(xg, axis=1)
        lo = jnp.pad(cs, ((0, 0), (w, 0), (0, 0)))[:, :S]
        cnt = jnp.minimum(t + 1, w).astype(jnp.float32)[None, :, None]
        outs.append((cs - lo) / cnt - xg)
    pooled = jnp.stack(outs, axis=2).astype(u.dtype)
    y = jnp.einsum('bsgc,gcd->bsgd', pooled, pool_w).reshape(B, S, POOL_WIDTH)
    return y * pool_scale


def setup_inputs(seed: int = 0) -> dict:
    key = jax.random.key(seed)
    ks = jax.random.split(key, 20)
    f32 = jnp.float32

    def lin(k, shape, fan_in):
        return jax.random.normal(k, shape, f32) * fan_in ** -0.5

    def gain(k, shape):
        return 1.0 + 0.05 * jax.random.normal(k, shape, f32)

    x = jax.random.normal(ks[0], (BATCH, SEQ, D_MODEL), f32)
    offs = jax.random.randint(ks[1], (BATCH, 1), 0, 1024, dtype=jnp.int32)
    positions = offs + jnp.arange(SEQ, dtype=jnp.int32)[None, :]
    return {
        "x": x,
        "positions": positions,
        "norm_mix_g": gain(ks[2], (DEPTH, D_MODEL)),
        "w_in": lin(ks[3], (DEPTH, D_MODEL, IN_WIDTH), D_MODEL),
        "q_norm_g": gain(ks[4], (DEPTH, Q_LORA)),
        "w_q_up": lin(ks[5], (DEPTH, Q_LORA, ATTN_HEADS * QK_DIM), Q_LORA),
        "kv_norm_g": gain(ks[6], (DEPTH, KV_LORA)),
        "w_kv_up": lin(ks[7], (DEPTH, KV_LORA, ATTN_HEADS * (NOPE_DIM + V_DIM)), KV_LORA),
        "pool_w": lin(ks[8], (DEPTH, POOL_GROUPS, POOL_CH, POOL_CH), POOL_CH),
        "pool_scale": gain(ks[9], (DEPTH, POOL_WIDTH)),
        "w_out": lin(ks[10], (DEPTH, MIX_WIDTH, D_MODEL), MIX_WIDTH),
        "norm_mlp_g": gain(ks[11], (DEPTH, D_MODEL)),
        "w_mlp_up": lin(ks[12], (DEPTH, D_MODEL, D_FF), D_MODEL),
        "w_mlp_down": lin(ks[13], (DEPTH, D_FF, D_MODEL), D_FF),
        "norm_final_g": gain(ks[14], (D_MODEL,)),
    }


def reference(x, positions, norm_mix_g, w_in, q_norm_g, w_q_up, kv_norm_g, w_kv_up,
              pool_w, pool_scale, w_out, norm_mlp_g, w_mlp_up, w_mlp_down, norm_final_g):
    h = x
    for l in range(DEPTH):
        n = rmsnorm(h, norm_mix_g[l])
        proj = n @ w_in[l]
        c_q = proj[..., :Q_LORA]
        c_kv = proj[..., Q_LORA:Q_LORA + KV_LORA]
        k_r = proj[..., Q_LORA + KV_LORA:Q_LORA + KV_LORA + ROPE_DIM]
        u = proj[..., Q_LORA + KV_LORA + ROPE_DIM:]
        a = mla(c_q, c_kv, k_r, q_norm_g[l], w_q_up[l], kv_norm_g[l], w_kv_up[l], positions)
        p = multiscale_pool(u, pool_w[l], pool_scale[l])
        h = h + jnp.concatenate([a, p], axis=-1) @ w_out[l]
        m = rmsnorm(h, norm_mlp_g[l]) @ w_mlp_up[l]
        h = h + jnp.square(jax.nn.relu(m)) @ w_mlp_down[l]
    return rmsnorm(h, norm_final_g)
```

```python
import functools
import math

import jax
import jax.numpy as jnp
from jax import lax
from jax.experimental import pallas as pl
from jax.experimental.pallas import tpu as pltpu

D_MODEL = 1024
HEADS = 4
NOPE = 128
ROPE = 64
VDIM = 128
QK = NOPE + ROPE
ATTN_W = HEADS * VDIM
Q_LORA = 384
KV_LORA = 256
GROUPS = 4
WINDOWS = (2, 4, 8, 16)
POOL_W = 512
POOL_CH = 128
D_FF = 4096
ROPE_THETA = 10000.0
EPS = 1e-6
HALO = 16
LANES = 128
VMEM_LIMIT = 56 * 1024 * 1024

F32 = jnp.float32
BF16 = jnp.bfloat16


def _rmsnorm(x, g):
    ms = jnp.mean(x * x, axis=-1, keepdims=True)
    return x * lax.rsqrt(ms + EPS) * g


def _front_kernel(x_ref, pos_ref, invf_ref, g_mix_ref, w_in_ref, g_q_ref, w_q_ref, g_kv_ref,
                  w_kv_ref, pool_w_ref, pool_scale_ref,
                  q_ref, k_ref, v_ref, p_ref, halo_ref, *, tm, tiles_per_seq, q_scale):
    i = pl.program_id(0)
    tile_in_seq = i % tiles_per_seq

    n = _rmsnorm(x_ref[...], g_mix_ref[...]).astype(BF16)
    proj = jnp.dot(n, w_in_ref[...], preferred_element_type=F32)
    cq = _rmsnorm(proj[:, :Q_LORA], g_q_ref[...]).astype(BF16)
    ckv = _rmsnorm(proj[:, Q_LORA:Q_LORA + KV_LORA], g_kv_ref[...]).astype(BF16)
    u_off = Q_LORA + KV_LORA
    u = proj[:, u_off:u_off + POOL_W]
    kr = proj[:, u_off + POOL_W:]

    ang = pos_ref[...].astype(F32) * invf_ref[...]
    lane = lax.broadcasted_iota(jnp.int32, (1, LANES), 1)
    cosv = jnp.cos(ang)
    sinv = jnp.sin(ang)
    tab = jnp.where(lane < 64, cosv, jnp.where(lane < 96, -sinv, sinv))

    def rope(pair):
        t = pair * tab
        return (t + pltpu.roll(t, 64, 1))[:, :ROPE]

    q = jnp.dot(cq, w_q_ref[...], preferred_element_type=F32)
    kv = jnp.dot(ckv, w_kv_ref[...], preferred_element_type=F32)
    k_pe = rope(kr).astype(BF16)
    for h in range(HEADS):
        q_ref[h, :, :NOPE] = (q[:, h * NOPE:(h + 1) * NOPE] * q_scale).astype(BF16)
        pe = rope(q[:, HEADS * NOPE + h * LANES:HEADS * NOPE + (h + 1) * LANES])
        q_ref[h, :, NOPE:] = (pe * q_scale).astype(BF16)
        k_ref[h, :, :NOPE] = kv[:, h * NOPE:(h + 1) * NOPE].astype(BF16)
        k_ref[h, :, NOPE:] = k_pe
    v_ref[...] = kv[:, HEADS * NOPE:].astype(BF16)

    @pl.when(tile_in_seq == 0)
    def _():
        halo_ref[...] = jnp.zeros_like(halo_ref)

    prev = halo_ref[...]
    halo_ref[...] = u[tm - HALO:, :]
    t_seq = tile_in_seq * tm + lax.broadcasted_iota(jnp.int32, (tm, 1), 0)
    for g, w in enumerate(WINDOWS):
        ug = u[:, g * POOL_CH:(g + 1) * POOL_CH]
        s = jnp.concatenate([prev[:, g * POOL_CH:(g + 1) * POOL_CH], ug], axis=0)
        for step in range(g + 1):
            s = s + pltpu.roll(s, 1 << step, 0)
        inv_cnt = 1.0 / jnp.minimum(t_seq + 1, w).astype(F32)
        pooled = (s[HALO:, :] * inv_cnt - ug).astype(BF16)
        y = jnp.dot(pooled, pool_w_ref[g], preferred_element_type=F32)
        p_ref[:, g * POOL_CH:(g + 1) * POOL_CH] = (
            y * pool_scale_ref[:, g * POOL_CH:(g + 1) * POOL_CH]).astype(BF16)


def _const_spec(shape):
    nd = len(shape)
    return pl.BlockSpec(shape, lambda *_: (0,) * nd, pipeline_mode=pl.Buffered(1))


def _front(x2, pos2, invf, g_mix, w_in, g_q, w_q, g_kv, w_kv, pool_w, pool_scale, *, seq, tm):
    t = x2.shape[0]
    tiles_per_seq = seq // tm
    q_scale = (QK ** -0.5) * math.log2(math.e)
    kern = functools.partial(_front_kernel, tm=tm, tiles_per_seq=tiles_per_seq, q_scale=q_scale)
    return pl.pallas_call(
        kern,
        grid=(t // tm,),
        in_specs=[
            pl.BlockSpec((tm, D_MODEL), lambda i: (i, 0)),
            pl.BlockSpec((tm, 1), lambda i: (i, 0)),
            _const_spec(invf.shape), _const_spec(g_mix.shape), _const_spec(w_in.shape),
            _const_spec(g_q.shape), _const_spec(w_q.shape), _const_spec(g_kv.shape),
            _const_spec(w_kv.shape), _const_spec(pool_w.shape), _const_spec(pool_scale.shape),
        ],
        out_specs=[
            pl.BlockSpec((HEADS, tm, QK), lambda i: (0, i, 0)),
            pl.BlockSpec((HEADS, tm, QK), lambda i: (0, i, 0)),
            pl.BlockSpec((tm, ATTN_W), lambda i: (i, 0)),
            pl.BlockSpec((tm, POOL_W), lambda i: (i, 0)),
        ],
        out_shape=[
            jax.ShapeDtypeStruct((HEADS, t, QK), BF16),
            jax.ShapeDtypeStruct((HEADS, t, QK), BF16),
            jax.ShapeDtypeStruct((t, ATTN_W), BF16),
            jax.ShapeDtypeStruct((t, POOL_W), BF16),
        ],
        scratch_shapes=[pltpu.VMEM((HALO, POOL_W), F32)],
        compiler_params=pltpu.CompilerParams(
            dimension_semantics=("arbitrary",), vmem_limit_bytes=VMEM_LIMIT),
        name="front",
    )(x2, pos2, invf, g_mix, w_in, g_q, w_q, g_kv, w_kv, pool_w, pool_scale)


def _attn_kernel(q_ref, k_ref, v_ref, o_ref, *, tq, tk):
    i = pl.program_id(1)
    row = lax.broadcasted_iota(jnp.int32, (tq, tk), 0)
    col = lax.broadcasted_iota(jnp.int32, (tq, tk), 1)
    causal = row >= col

    for h in range(HEADS):
        q = q_ref[h]

        def block(j, carry, masked):
            m, l, acc = carry
            start = pl.multiple_of(j * tk, tk)
            k = k_ref[h, pl.ds(start, tk), :]
            v = v_ref[pl.ds(start, tk), h * VDIM:(h + 1) * VDIM]
            s = lax.dot_general(q, k, (((1,), (1,)), ((), ())), preferred_element_type=F32)
            if masked:
                s = jnp.where(causal, s, -jnp.inf)
            m_new = jnp.maximum(m, jnp.max(s, axis=-1, keepdims=True))
            alpha = jnp.exp2(m - m_new)
            p = jnp.exp2(s - m_new)
            l = alpha * l + jnp.sum(p, axis=-1, keepdims=True)
            acc = alpha * acc + jnp.dot(p.astype(BF16), v, preferred_element_type=F32)
            return m_new, l, acc

        init = (jnp.full((tq, 1), -jnp.inf, F32), jnp.zeros((tq, 1), F32),
                jnp.zeros((tq, VDIM), F32))
        carry = lax.fori_loop(0, i, functools.partial(block, masked=False), init)
        _, l, acc = block(i, carry, True)
        o_ref[:, h * VDIM:(h + 1) * VDIM] = (acc / l).astype(BF16)


def _attn(q, k, v, *, batch, seq, tq):
    t = v.shape[0]
    nq = seq // tq
    kern = functools.partial(_attn_kernel, tq=tq, tk=tq)
    return pl.pallas_call(
        kern,
        grid=(batch, nq),
        in_specs=[
            pl.BlockSpec((HEADS, tq, QK), lambda b, i: (0, b * nq + i, 0)),
            pl.BlockSpec((HEADS, seq, QK), lambda b, i: (0, b, 0)),
            pl.BlockSpec((seq, ATTN_W), lambda b, i: (b, 0)),
        ],
        out_specs=pl.BlockSpec((tq, ATTN_W), lambda b, i: (b * nq + i, 0)),
        out_shape=jax.ShapeDtypeStruct((t, ATTN_W), BF16),
        compiler_params=pltpu.CompilerParams(
            dimension_semantics=("arbitrary", "arbitrary"), vmem_limit_bytes=VMEM_LIMIT),
        name="attn",
    )(q, k, v)


def _back_kernel(x_ref, a_ref, p_ref, w_oa_ref, w_op_ref, g_mlp_ref, w_up_ref, w_down_ref,
                 g_fin_ref, o_ref, *, fc):
    h = (x_ref[...]
         + jnp.dot(a_ref[...], w_oa_ref[...], preferred_element_type=F32)
         + jnp.dot(p_ref[...], w_op_ref[...], preferred_element_type=F32))
    n = _rmsnorm(h, g_mlp_ref[...]).astype(BF16)
    acc = None
    for c in range(D_FF // fc):
        m = jnp.dot(n, w_up_ref[:, c * fc:(c + 1) * fc], preferred_element_type=F32)
        r = jnp.square(jnp.maximum(m, 0.0)).astype(BF16)
        d = jnp.dot(r, w_down_ref[c * fc:(c + 1) * fc, :], preferred_element_type=F32)
        acc = d if acc is None else acc + d
    o_ref[...] = _rmsnorm(h + acc, g_fin_ref[...])


def _back(x2, a, p, w_oa, w_op, g_mlp, w_up, w_down, g_fin, *, tm, fc):
    t = x2.shape[0]
    kern = functools.partial(_back_kernel, fc=fc)
    return pl.pallas_call(
        kern,
        grid=(t // tm,),
        in_specs=[
            pl.BlockSpec((tm, D_MODEL), lambda i: (i, 0)),
            pl.BlockSpec((tm, ATTN_W), lambda i: (i, 0)),
            pl.BlockSpec((tm, POOL_W), lambda i: (i, 0)),
            _const_spec(w_oa.shape), _const_spec(w_op.shape), _const_spec(g_mlp.shape),
            _const_spec(w_up.shape), _const_spec(w_down.shape), _const_spec(g_fin.shape),
        ],
        out_specs=pl.BlockSpec((tm, D_MODEL), lambda i: (i, 0)),
        out_shape=jax.ShapeDtypeStruct((t, D_MODEL), F32),
        compiler_params=pltpu.CompilerParams(
            dimension_semantics=("arbitrary",), vmem_limit_bytes=VMEM_LIMIT),
        name="back",
    )(x2, a, p, w_oa, w_op, g_mlp, w_up, w_down, g_fin)


def _half_swap(w):
    half = w.shape[-1] // 2
    return jnp.concatenate([w[..., half:], w[..., :half]], axis=-1)


def kernel(x, positions, norm_mix_g, w_in, q_norm_g, w_q_up, kv_norm_g, w_kv_up, pool_w,
           pool_scale, w_out, norm_mlp_g, w_mlp_up, w_mlp_down, norm_final_g):
    batch, seq, _ = x.shape
    t = batch * seq
    x2 = x.reshape(t, D_MODEL)
    pos2 = positions.reshape(t, 1)
    inv_freq = 1.0 / (ROPE_THETA ** (jnp.arange(0, ROPE, 2, dtype=F32) / ROPE))
    invf = jnp.tile(inv_freq, 4).reshape(1, LANES)

    assert norm_mix_g.shape[0] == 1, "single-layer block only"
    wi = w_in[0]
    kr_cols = wi[:, Q_LORA + KV_LORA:Q_LORA + KV_LORA + ROPE]
    w_in_l = jnp.concatenate(
        [wi[:, :Q_LORA + KV_LORA], wi[:, Q_LORA + KV_LORA + ROPE:], kr_cols,
         _half_swap(kr_cols)], axis=1).astype(BF16)
    wq = w_q_up[0].reshape(Q_LORA, HEADS, QK)
    wq_pe = wq[:, :, NOPE:]
    w_q_l = jnp.concatenate(
        [wq[:, :, :NOPE].reshape(Q_LORA, HEADS * NOPE),
         jnp.concatenate([wq_pe, _half_swap(wq_pe)], axis=-1).reshape(Q_LORA, HEADS * LANES)],
        axis=1).astype(BF16)
    wkv = w_kv_up[0].reshape(KV_LORA, HEADS, NOPE + VDIM)
    w_kv_l = jnp.concatenate(
        [wkv[:, :, :NOPE].reshape(KV_LORA, HEADS * NOPE),
         wkv[:, :, NOPE:].reshape(KV_LORA, HEADS * VDIM)], axis=1).astype(BF16)

    q, k, v, p = _front(
        x2, pos2, invf, norm_mix_g[0].reshape(1, D_MODEL), w_in_l,
        q_norm_g[0].reshape(1, Q_LORA), w_q_l, kv_norm_g[0].reshape(1, KV_LORA), w_kv_l,
        pool_w[0].astype(BF16), pool_scale[0].reshape(1, POOL_W), seq=seq, tm=512)
    a = _attn(q, k, v, batch=batch, seq=seq, tq=512)
    out = _back(
        x2, a, p, w_out[0, :ATTN_W].astype(BF16), w_out[0, ATTN_W:].astype(BF16),
        norm_mlp_g[0].reshape(1, D_MODEL), w_mlp_up[0].astype(BF16),
        w_mlp_down[0].astype(BF16), norm_final_g.reshape(1, D_MODEL), tm=512, fc=1024)
    return out.reshape(batch, seq, D_MODEL)
```

```python
import functools
import math

import jax
import jax.numpy as jnp
from jax import lax
from jax.experimental import pallas as pl
from jax.experimental.pallas import tpu as pltpu

D_MODEL = 1024
HEADS = 4
NOPE = 128
ROPE = 64
VDIM = 128
QK = NOPE + ROPE
ATTN_W = HEADS * VDIM
Q_LORA = 384
KV_LORA = 256
GROUPS = 4
WINDOWS = (2, 4, 8, 16)
POOL_W = 512
POOL_CH = 128
D_FF = 4096
ROPE_THETA = 10000.0
EPS = 1e-6
HALO = 16
LANES = 128
LOOKAHEAD = 3
VMEM_LIMIT = 56 * 1024 * 1024

F32 = jnp.float32
BF16 = jnp.bfloat16


def _rmsnorm(x, g):
    ms = jnp.mean(x * x, axis=-1, keepdims=True)
    return x * lax.rsqrt(ms + EPS) * g


def _front_kernel(x_ref, pos_ref, invf_ref, g_mix_ref, w_in_ref, g_q_ref, w_q_ref, g_kv_ref,
                  w_k_ref, w_vt_ref, pool_w_ref, pool_scale_ref,
                  q_ref, k_ref, vt_ref, p_ref, halo_ref, *, tm, tiles_per_seq, q_scale):
    i = pl.program_id(0)
    tile_in_seq = i % tiles_per_seq

    n = _rmsnorm(x_ref[...], g_mix_ref[...]).astype(BF16)
    proj = jnp.dot(n, w_in_ref[...], preferred_element_type=F32)
    cq = _rmsnorm(proj[:, :Q_LORA], g_q_ref[...]).astype(BF16)
    ckv = _rmsnorm(proj[:, Q_LORA:Q_LORA + KV_LORA], g_kv_ref[...]).astype(BF16)
    u_off = Q_LORA + KV_LORA
    u = proj[:, u_off:u_off + POOL_W]
    kr = proj[:, u_off + POOL_W:]

    ang = pos_ref[...].astype(F32) * invf_ref[...]
    lane = lax.broadcasted_iota(jnp.int32, (1, LANES), 1)
    cosv = jnp.cos(ang)
    sinv = jnp.sin(ang)
    tab = jnp.where(lane < 64, cosv, jnp.where(lane < 96, -sinv, sinv))

    def rope(pair):
        t = pair * tab
        return (t + pltpu.roll(t, 64, 1))[:, :ROPE]

    q = jnp.dot(cq, w_q_ref[...], preferred_element_type=F32)
    kn = jnp.dot(ckv, w_k_ref[...], preferred_element_type=F32)
    vt_ref[...] = lax.dot_general(w_vt_ref[...], ckv, (((1,), (1,)), ((), ())),
                                  preferred_element_type=F32).astype(BF16)
    k_pe = rope(kr).astype(BF16)
    for h in range(HEADS):
        q_ref[h, :, :NOPE] = (q[:, h * NOPE:(h + 1) * NOPE] * q_scale).astype(BF16)
        pe = rope(q[:, HEADS * NOPE + h * LANES:HEADS * NOPE + (h + 1) * LANES])
        q_ref[h, :, NOPE:] = (pe * q_scale).astype(BF16)
        k_ref[h, :, :NOPE] = kn[:, h * NOPE:(h + 1) * NOPE].astype(BF16)
        k_ref[h, :, NOPE:] = k_pe

    @pl.when(tile_in_seq == 0)
    def _():
        halo_ref[...] = jnp.zeros_like(halo_ref)

    prev = halo_ref[...]
    halo_ref[...] = u[tm - HALO:, :]
    t_seq = tile_in_seq * tm + lax.broadcasted_iota(jnp.int32, (tm, 1), 0)
    for g, w in enumerate(WINDOWS):
        ug = u[:, g * POOL_CH:(g + 1) * POOL_CH]
        s = jnp.concatenate([prev[:, g * POOL_CH:(g + 1) * POOL_CH], ug], axis=0)
        for step in range(g + 1):
            s = s + pltpu.roll(s, 1 << step, 0)
        inv_cnt = 1.0 / jnp.minimum(t_seq + 1, w).astype(F32)
        pooled = (s[HALO:, :] * inv_cnt - ug).astype(BF16)
        y = jnp.dot(pooled, pool_w_ref[g], preferred_element_type=F32)
        p_ref[:, g * POOL_CH:(g + 1) * POOL_CH] = (
            y * pool_scale_ref[:, g * POOL_CH:(g + 1) * POOL_CH]).astype(BF16)


def _const_spec(shape):
    nd = len(shape)
    return pl.BlockSpec(shape, lambda *_: (0,) * nd, pipeline_mode=pl.Buffered(1))


def _front(x2, pos2, invf, g_mix, w_in, g_q, w_q, g_kv, w_k, w_vt, pool_w, pool_scale, *, seq, tm):
    t = x2.shape[0]
    tiles_per_seq = seq // tm
    q_scale = (QK ** -0.5) * math.log2(math.e)
    kern = functools.partial(_front_kernel, tm=tm, tiles_per_seq=tiles_per_seq, q_scale=q_scale)
    return pl.pallas_call(
        kern,
        grid=(t // tm,),
        in_specs=[
            pl.BlockSpec((tm, D_MODEL), lambda i: (i, 0)),
            pl.BlockSpec((tm, 1), lambda i: (i, 0)),
            _const_spec(invf.shape), _const_spec(g_mix.shape), _const_spec(w_in.shape),
            _const_spec(g_q.shape), _const_spec(w_q.shape), _const_spec(g_kv.shape),
            _const_spec(w_k.shape), _const_spec(w_vt.shape), _const_spec(pool_w.shape),
            _const_spec(pool_scale.shape),
        ],
        out_specs=[
            pl.BlockSpec((HEADS, tm, QK), lambda i: (0, i, 0)),
            pl.BlockSpec((HEADS, tm, QK), lambda i: (0, i, 0)),
            pl.BlockSpec((ATTN_W, tm), lambda i: (0, i)),
            pl.BlockSpec((tm, POOL_W), lambda i: (i, 0)),
        ],
        out_shape=[
            jax.ShapeDtypeStruct((HEADS, t, QK), BF16),
            jax.ShapeDtypeStruct((HEADS, t, QK), BF16),
            jax.ShapeDtypeStruct((ATTN_W, t), BF16),
            jax.ShapeDtypeStruct((t, POOL_W), BF16),
        ],
        scratch_shapes=[pltpu.VMEM((HALO, POOL_W), F32)],
        compiler_params=pltpu.CompilerParams(
            dimension_semantics=("arbitrary",), vmem_limit_bytes=VMEM_LIMIT),
        name="front",
    )(x2, pos2, invf, g_mix, w_in, g_q, w_q, g_kv, w_k, w_vt, pool_w, pool_scale)


def _attn_kernel(q_ref, k_ref, vt_ref, o_ref, *, tq, tk):
    i = pl.program_id(1)
    heads = range(HEADS)

    def scores(h, j):
        start = pl.multiple_of(j * tk, tk)
        return lax.dot_general(k_ref[h, pl.ds(start, tk), :], q_ref[h],
                               (((1,), (1,)), ((), ())), preferred_element_type=F32)

    def softmax(s, m, l, masked):
        if masked:
            kpos = lax.broadcasted_iota(jnp.int32, (tk, tq), 0)
            qpos = lax.broadcasted_iota(jnp.int32, (tk, tq), 1)
            s = jnp.where(qpos >= kpos, s, -jnp.inf)
        m_new = jnp.maximum(m, jnp.max(s, axis=0, keepdims=True))
        alpha = jnp.exp2(m - m_new)
        p = jnp.exp2(s - m_new)
        l = alpha * l + jnp.sum(p, axis=0, keepdims=True)
        return m_new, l, alpha, p.astype(BF16)

    def values(h, j, p, alpha, acc):
        start = pl.multiple_of(j * tk, tk)
        vt = vt_ref[h * VDIM:(h + 1) * VDIM, pl.ds(start, tk)]
        return alpha * acc + jnp.dot(vt, p, preferred_element_type=F32)

    def block(j, cs, masked):
        ss = {h: scores(h, j) for h in range(min(LOOKAHEAD, HEADS))}
        new_c = []
        for h in heads:
            m, l, acc = cs[h]
            if h + LOOKAHEAD < HEADS:
                ss[h + LOOKAHEAD] = scores(h + LOOKAHEAD, j)
            m, l, alpha, p = softmax(ss.pop(h), m, l, masked)
            new_c.append((m, l, values(h, j, p, alpha, acc)))
        return tuple(new_c)

    init = tuple((jnp.full((1, tq), -jnp.inf, F32), jnp.zeros((1, tq), F32),
                  jnp.zeros((VDIM, tq), F32)) for _ in heads)
    cs = lax.fori_loop(0, i, functools.partial(block, masked=False), init)
    cs = block(i, cs, True)
    for h in heads:
        _, l, acc = cs[h]
        o_ref[:, h * VDIM:(h + 1) * VDIM] = (acc / l).T.astype(BF16)


def _attn(q, k, vt, *, batch, seq, tq):
    t = vt.shape[1]
    nq = seq // tq
    kern = functools.partial(_attn_kernel, tq=tq, tk=tq)
    return pl.pallas_call(
        kern,
        grid=(batch, nq),
        in_specs=[
            pl.BlockSpec((HEADS, tq, QK), lambda b, i: (0, b * nq + i, 0)),
            pl.BlockSpec((HEADS, seq, QK), lambda b, i: (0, b, 0)),
            pl.BlockSpec((ATTN_W, seq), lambda b, i: (0, b)),
        ],
        out_specs=pl.BlockSpec((tq, ATTN_W), lambda b, i: (b * nq + i, 0)),
        out_shape=jax.ShapeDtypeStruct((t, ATTN_W), BF16),
        compiler_params=pltpu.CompilerParams(
            dimension_semantics=("arbitrary", "arbitrary"), vmem_limit_bytes=VMEM_LIMIT),
        name="attn",
    )(q, k, vt)


def _back_kernel(x_ref, a_ref, p_ref, w_oa_ref, w_op_ref, g_mlp_ref, w_up_ref, w_down_ref,
                 g_fin_ref, o_ref, *, fc):
    h = (x_ref[...]
         + jnp.dot(a_ref[...], w_oa_ref[...], preferred_element_type=F32)
         + jnp.dot(p_ref[...], w_op_ref[...], preferred_element_type=F32))
    n = _rmsnorm(h, g_mlp_ref[...]).astype(BF16)
    acc = None
    for c in range(D_FF // fc):
        m = jnp.dot(n, w_up_ref[:, c * fc:(c + 1) * fc], preferred_element_type=F32)
        r = jnp.square(jnp.maximum(m, 0.0)).astype(BF16)
        d = jnp.dot(r, w_down_ref[c * fc:(c + 1) * fc, :], preferred_element_type=F32)
        acc = d if acc is None else acc + d
    o_ref[...] = _rmsnorm(h + acc, g_fin_ref[...])


def _back(x2, a, p, w_oa, w_op, g_mlp, w_up, w_down, g_fin, *, tm, fc):
    t = x2.shape[0]
    kern = functools.partial(_back_kernel, fc=fc)
    return pl.pallas_call(
        kern,
        grid=(t // tm,),
        in_specs=[
            pl.BlockSpec((tm, D_MODEL), lambda i: (i, 0)),
            pl.BlockSpec((tm, ATTN_W), lambda i: (i, 0)),
            pl.BlockSpec((tm, POOL_W), lambda i: (i, 0)),
            _const_spec(w_oa.shape), _const_spec(w_op.shape), _const_spec(g_mlp.shape),
            _const_spec(w_up.shape), _const_spec(w_down.shape), _const_spec(g_fin.shape),
        ],
        out_specs=pl.BlockSpec((tm, D_MODEL), lambda i: (i, 0)),
        out_shape=jax.ShapeDtypeStruct((t, D_MODEL), F32),
        compiler_params=pltpu.CompilerParams(
            dimension_semantics=("arbitrary",), vmem_limit_bytes=VMEM_LIMIT),
        name="back",
    )(x2, a, p, w_oa, w_op, g_mlp, w_up, w_down, g_fin)


def _half_swap(w):
    half = w.shape[-1] // 2
    return jnp.concatenate([w[..., half:], w[..., :half]], axis=-1)


def kernel(x, positions, norm_mix_g, w_in, q_norm_g, w_q_up, kv_norm_g, w_kv_up, pool_w,
           pool_scale, w_out, norm_mlp_g, w_mlp_up, w_mlp_down, norm_final_g):
    batch, seq, _ = x.shape
    t = batch * seq
    x2 = x.reshape(t, D_MODEL)
    pos2 = positions.reshape(t, 1)
    inv_freq = 1.0 / (ROPE_THETA ** (jnp.arange(0, ROPE, 2, dtype=F32) / ROPE))
    invf = jnp.tile(inv_freq, 4).reshape(1, LANES)

    assert norm_mix_g.shape[0] == 1, "single-layer block only"
    wi = w_in[0]
    kr_cols = wi[:, Q_LORA + KV_LORA:Q_LORA + KV_LORA + ROPE]
    w_in_l = jnp.concatenate(
        [wi[:, :Q_LORA + KV_LORA], wi[:, Q_LORA + KV_LORA + ROPE:], kr_cols,
         _half_swap(kr_cols)], axis=1).astype(BF16)
    wq = w_q_up[0].reshape(Q_LORA, HEADS, QK)
    wq_pe = wq[:, :, NOPE:]
    w_q_l = jnp.concatenate(
        [wq[:, :, :NOPE].reshape(Q_LORA, HEADS * NOPE),
         jnp.concatenate([wq_pe, _half_swap(wq_pe)], axis=-1).reshape(Q_LORA, HEADS * LANES)],
        axis=1).astype(BF16)
    wkv = w_kv_up[0].reshape(KV_LORA, HEADS, NOPE + VDIM)
    w_k_l = wkv[:, :, :NOPE].reshape(KV_LORA, HEADS * NOPE).astype(BF16)
    w_vt_l = wkv[:, :, NOPE:].reshape(KV_LORA, HEADS * VDIM).T.astype(BF16)

    q, k, vt, p = _front(
        x2, pos2, invf, norm_mix_g[0].reshape(1, D_MODEL), w_in_l,
        q_norm_g[0].reshape(1, Q_LORA), w_q_l, kv_norm_g[0].reshape(1, KV_LORA), w_k_l, w_vt_l,
        pool_w[0].astype(BF16), pool_scale[0].reshape(1, POOL_W), seq=seq, tm=512)
    a = _attn(q, k, vt, batch=batch, seq=seq, tq=512)
    out = _back(
        x2, a, p, w_out[0, :ATTN_W].astype(BF16), w_out[0, ATTN_W:].astype(BF16),
        norm_mlp_g[0].reshape(1, D_MODEL), w_mlp_up[0].astype(BF16),
        w_mlp_down[0].astype(BF16), norm_final_g.reshape(1, D_MODEL), tm=512, fc=1024)
    return out.reshape(batch, seq, D_MODEL)
```

```python
import functools
import math

import jax
import jax.numpy as jnp
from jax import lax
from jax.experimental import pallas as pl
from jax.experimental.pallas import tpu as pltpu

D_MODEL = 1024
HEADS = 4
NOPE = 128
ROPE = 64
VDIM = 128
QK = NOPE + ROPE
ATTN_W = HEADS * VDIM
Q_LORA = 384
KV_LORA = 256
GROUPS = 4
WINDOWS = (2, 4, 8, 16)
POOL_W = 512
POOL_CH = 128
D_FF = 4096
ROPE_THETA = 10000.0
EPS = 1e-6
HALO = 16
LANES = 128
LOOKAHEAD = 3
VMEM_LIMIT = 56 * 1024 * 1024
FRONT_TM = 512
FRONT_SUB = 256
ATTN_TQ = 512
BACK_TM = 512
BACK_FC = 1024

F32 = jnp.float32
BF16 = jnp.bfloat16


def _rmsnorm(x, g):
    ms = jnp.mean(x * x, axis=-1, keepdims=True)
    return x * lax.rsqrt(ms + EPS) * g


def _front_kernel(x_ref, pos_ref, invf_ref, g_mix_ref, w_in_ref, g_q_ref, w_q_ref, g_kv_ref,
                  w_k_ref, w_vt_ref, pool_w_ref, pool_scale_ref,
                  q_ref, k_ref, vt_ref, p_ref, halo_ref, tab_ref, *, tm, sub, tiles_per_seq,
                  q_scale):
    i = pl.program_id(0)
    tile_in_seq = i % tiles_per_seq
    rows = [pl.ds(r * sub, sub) for r in range(tm // sub)]
    lane = lax.broadcasted_iota(jnp.int32, (1, LANES), 1)
    u_off = Q_LORA + KV_LORA

    @pl.when(tile_in_seq == 0)
    def _():
        halo_ref[...] = jnp.zeros_like(halo_ref)

    quarter = tm // 4
    ang = pos_ref[...].astype(F32) * invf_ref[...]
    cos4 = jnp.cos(ang)
    sin4 = jnp.sin(ang)
    for k in range(4):
        def group_to(x, dst, k=k):
            shift = (32 * (dst - k)) % LANES
            return pltpu.roll(x, shift, 1) if shift else x
        cos_k = jnp.where(lane < 32, group_to(cos4, 0), group_to(cos4, 1))
        sin_k = jnp.where(lane < 96, -group_to(sin4, 2), group_to(sin4, 3))
        tab_ref[k * quarter:(k + 1) * quarter, :] = jnp.where(lane < 64, cos_k, sin_k)

    n = [_rmsnorm(x_ref[r, :], g_mix_ref[...]).astype(BF16) for r in rows]
    proj = [jnp.dot(ns, w_in_ref[...], preferred_element_type=F32) for ns in n]
    cq = [_rmsnorm(pr[:, :Q_LORA], g_q_ref[...]).astype(BF16) for pr in proj]
    ckv = [_rmsnorm(pr[:, Q_LORA:u_off], g_kv_ref[...]).astype(BF16) for pr in proj]
    u = [pr[:, u_off:u_off + POOL_W] for pr in proj]

    def rope(pair, tab):
        t = pair * tab
        return (t + pltpu.roll(t, 64, 1))[:, :ROPE]

    for r, pr, cqs, ckvs in zip(rows, proj, cq, ckv):
        tab = tab_ref[r, :]
        q = jnp.dot(cqs, w_q_ref[...], preferred_element_type=F32)
        kn = jnp.dot(ckvs, w_k_ref[...], preferred_element_type=F32)
        vt_ref[:, r] = lax.dot_general(w_vt_ref[...], ckvs, (((1,), (1,)), ((), ())),
                                       preferred_element_type=F32).astype(BF16)
        k_pe = rope(pr[:, u_off + POOL_W:], tab).astype(BF16)
        for h in range(HEADS):
            q_ref[h, r, :NOPE] = (q[:, h * NOPE:(h + 1) * NOPE] * q_scale).astype(BF16)
            pe = rope(q[:, HEADS * NOPE + h * LANES:HEADS * NOPE + (h + 1) * LANES], tab)
            q_ref[h, r, NOPE:] = (pe * q_scale).astype(BF16)
            k_ref[h, r, :NOPE] = kn[:, h * NOPE:(h + 1) * NOPE].astype(BF16)
            k_ref[h, r, NOPE:] = k_pe

    prev = halo_ref[...]
    halo_ref[...] = u[-1][sub - HALO:, :]
    for si, (r, us) in enumerate(zip(rows, u)):
        t_seq = (tile_in_seq * tm + si * sub) + lax.broadcasted_iota(jnp.int32, (sub, 1), 0)
        for g, w in enumerate(WINDOWS):
            cols = slice(g * POOL_CH, (g + 1) * POOL_CH)
            ug = us[:, cols]
            s = jnp.concatenate([prev[:, cols], ug], axis=0)
            for step in range(g + 1):
                s = s + pltpu.roll(s, 1 << step, 0)
            inv_cnt = 1.0 / jnp.minimum(t_seq + 1, w).astype(F32)
            pooled = (s[HALO:, :] * inv_cnt - ug).astype(BF16)
            y = jnp.dot(pooled, pool_w_ref[g], preferred_element_type=F32)
            p_ref[r, cols] = (y * pool_scale_ref[:, cols]).astype(BF16)
        prev = us[sub - HALO:, :]


def _const_spec(shape):
    nd = len(shape)
    return pl.BlockSpec(shape, lambda *_: (0,) * nd, pipeline_mode=pl.Buffered(1))


def _front(x2, pos2, invf, g_mix, w_in, g_q, w_q, g_kv, w_k, w_vt, pool_w, pool_scale, *, seq, tm,
           sub):
    t = x2.shape[0]
    tiles_per_seq = seq // tm
    q_scale = (QK ** -0.5) * math.log2(math.e)
    kern = functools.partial(_front_kernel, tm=tm, sub=sub, tiles_per_seq=tiles_per_seq,
                             q_scale=q_scale)
    return pl.pallas_call(
        kern,
        grid=(t // tm,),
        in_specs=[
            pl.BlockSpec((tm, D_MODEL), lambda i: (i, 0)),
            pl.BlockSpec((tm // 4, LANES), lambda i: (i, 0)),
            _const_spec(invf.shape), _const_spec(g_mix.shape), _const_spec(w_in.shape),
            _const_spec(g_q.shape), _const_spec(w_q.shape), _const_spec(g_kv.shape),
            _const_spec(w_k.shape), _const_spec(w_vt.shape), _const_spec(pool_w.shape),
            _const_spec(pool_scale.shape),
        ],
        out_specs=[
            pl.BlockSpec((HEADS, tm, QK), lambda i: (0, i, 0)),
            pl.BlockSpec((HEADS, tm, QK), lambda i: (0, i, 0)),
            pl.BlockSpec((ATTN_W, tm), lambda i: (0, i)),
            pl.BlockSpec((tm, POOL_W), lambda i: (i, 0)),
        ],
        out_shape=[
            jax.ShapeDtypeStruct((HEADS, t, QK), BF16),
            jax.ShapeDtypeStruct((HEADS, t, QK), BF16),
            jax.ShapeDtypeStruct((ATTN_W, t), BF16),
            jax.ShapeDtypeStruct((t, POOL_W), BF16),
        ],
        scratch_shapes=[pltpu.VMEM((HALO, POOL_W), F32), pltpu.VMEM((tm, LANES), F32)],
        compiler_params=pltpu.CompilerParams(
            dimension_semantics=("arbitrary",), vmem_limit_bytes=VMEM_LIMIT),
        name="front",
    )(x2, pos2, invf, g_mix, w_in, g_q, w_q, g_kv, w_k, w_vt, pool_w, pool_scale)


def _attn_kernel(q_ref, k_ref, vt_ref, o_ref, acc_ref, *, tq, tk):
    i = pl.program_id(1)
    half = tk // 2
    tri = (lax.broadcasted_iota(jnp.int32, (half, half), 1)
           >= lax.broadcasted_iota(jnp.int32, (half, half), 0))

    def scores(item):
        h, k0, rows, q0, cols, _ = item
        return lax.dot_general(k_ref[h, pl.ds(k0, rows), :], q_ref[h, q0:q0 + cols, :],
                               (((1,), (1,)), ((), ())), preferred_element_type=F32)

    def run(items, state, first=False):
        state = list(state)
        pending = {n: scores(items[n]) for n in range(min(LOOKAHEAD, len(items)))}
        for n, item in enumerate(items):
            if n + LOOKAHEAD < len(items):
                pending[n + LOOKAHEAD] = scores(items[n + LOOKAHEAD])
            h, k0, rows, q0, cols, masked = item
            s = pending.pop(n)
            if masked:
                lower = jnp.where(tri, s[:, :half], -jnp.inf)
                s = lower if cols == half else jnp.concatenate([lower, s[:, half:]], axis=1)
            vt = vt_ref[h * VDIM:(h + 1) * VDIM, pl.ds(k0, rows)]
            if first and state[h] is None:
                assert cols == tq
                m = jnp.max(s, axis=0, keepdims=True)
                p = jnp.exp2(s - m)
                state[h] = (m, jnp.sum(p, axis=0, keepdims=True))
                acc_ref[h] = jnp.dot(vt, p.astype(BF16), preferred_element_type=F32)
                continue
            m_all, l_all = state[h]
            m_old, l_old = m_all[:, q0:q0 + cols], l_all[:, q0:q0 + cols]
            m = jnp.maximum(m_old, jnp.max(s, axis=0, keepdims=True))
            alpha = jnp.exp2(m_old - m)
            p = jnp.exp2(s - m)
            l = alpha * l_old + jnp.sum(p, axis=0, keepdims=True)
            acc_ref[h, :, q0:q0 + cols] = (
                alpha * acc_ref[h, :, q0:q0 + cols]
                + jnp.dot(vt, p.astype(BF16), preferred_element_type=F32))
            if cols != tq:
                assert q0 + cols == tq
                m = jnp.concatenate([m_all[:, :q0], m], axis=1)
                l = jnp.concatenate([l_all[:, :q0], l], axis=1)
            state[h] = (m, l)
        return tuple(state)

    d0 = pl.multiple_of(i * tk, tk)
    d1 = pl.multiple_of(i * tk + half, half)
    state = run([(h, d0, half, 0, tq, True) for h in range(HEADS)]
                + [(h, d1, half, half, tq - half, True) for h in range(HEADS)],
                (None,) * HEADS, first=True)

    def below_diagonal(j, state):
        k0 = pl.multiple_of(j * tk, tk)
        return run([(h, k0, tk, 0, tq, False) for h in range(HEADS)], state)

    state = lax.fori_loop(0, i, below_diagonal, state)
    for h in range(HEADS):
        o_ref[:, h * VDIM:(h + 1) * VDIM] = (acc_ref[h] / state[h][1]).T.astype(BF16)


def _attn(q, k, vt, *, batch, seq, tq):
    t = vt.shape[1]
    nq = seq // tq
    kern = functools.partial(_attn_kernel, tq=tq, tk=tq)
    return pl.pallas_call(
        kern,
        grid=(batch, nq),
        in_specs=[
            pl.BlockSpec((HEADS, tq, QK), lambda b, i: (0, b * nq + i, 0)),
            pl.BlockSpec((HEADS, seq, QK), lambda b, i: (0, b, 0)),
            pl.BlockSpec((ATTN_W, seq), lambda b, i: (0, b)),
        ],
        out_specs=pl.BlockSpec((tq, ATTN_W), lambda b, i: (b * nq + i, 0)),
        out_shape=jax.ShapeDtypeStruct((t, ATTN_W), BF16),
        scratch_shapes=[pltpu.VMEM((HEADS, VDIM, tq), F32)],
        compiler_params=pltpu.CompilerParams(
            dimension_semantics=("arbitrary", "arbitrary"), vmem_limit_bytes=VMEM_LIMIT),
        name="attn",
    )(q, k, vt)


def _back_kernel(x_ref, a_ref, p_ref, w_oa_ref, w_op_ref, g_mlp_ref, w_up_ref, w_down_ref,
                 g_fin_ref, o_ref, *, fc):
    h = (x_ref[...]
         + jnp.dot(a_ref[...], w_oa_ref[...], preferred_element_type=F32)
         + jnp.dot(p_ref[...], w_op_ref[...], preferred_element_type=F32))
    n = _rmsnorm(h, g_mlp_ref[...]).astype(BF16)
    acc = None
    for c in range(D_FF // fc):
        m = jnp.dot(n, w_up_ref[:, c * fc:(c + 1) * fc], preferred_element_type=F32)
        r = jnp.square(jnp.maximum(m, 0.0)).astype(BF16)
        d = jnp.dot(r, w_down_ref[c * fc:(c + 1) * fc, :], preferred_element_type=F32)
        acc = d if acc is None else acc + d
    o_ref[...] = _rmsnorm(h + acc, g_fin_ref[...])


def _back(x2, a, p, w_oa, w_op, g_mlp, w_up, w_down, g_fin, *, tm, fc):
    t = x2.shape[0]
    kern = functools.partial(_back_kernel, fc=fc)
    return pl.pallas_call(
        kern,
        grid=(t // tm,),
        in_specs=[
            pl.BlockSpec((tm, D_MODEL), lambda i: (i, 0)),
            pl.BlockSpec((tm, ATTN_W), lambda i: (i, 0)),
            pl.BlockSpec((tm, POOL_W), lambda i: (i, 0)),
            _const_spec(w_oa.shape), _const_spec(w_op.shape), _const_spec(g_mlp.shape),
            _const_spec(w_up.shape), _const_spec(w_down.shape), _const_spec(g_fin.shape),
        ],
        out_specs=pl.BlockSpec((tm, D_MODEL), lambda i: (i, 0)),
        out_shape=jax.ShapeDtypeStruct((t, D_MODEL), F32),
        compiler_params=pltpu.CompilerParams(
            dimension_semantics=("arbitrary",), vmem_limit_bytes=VMEM_LIMIT),
        name="back",
    )(x2, a, p, w_oa, w_op, g_mlp, w_up, w_down, g_fin)


def _half_swap(w):
    half = w.shape[-1] // 2
    return jnp.concatenate([w[..., half:], w[..., :half]], axis=-1)


def kernel(x, positions, norm_mix_g, w_in, q_norm_g, w_q_up, kv_norm_g, w_kv_up, pool_w,
           pool_scale, w_out, norm_mlp_g, w_mlp_up, w_mlp_down, norm_final_g):
    batch, seq, _ = x.shape
    t = batch * seq
    x2 = x.reshape(t, D_MODEL)
    pos2 = jnp.broadcast_to(
        positions.reshape(t // FRONT_TM, 4, FRONT_TM // 4, 1).transpose(0, 2, 1, 3),
        (t // FRONT_TM, FRONT_TM // 4, 4, LANES // 4)).reshape(t // 4, LANES)
    inv_freq = 1.0 / (ROPE_THETA ** (jnp.arange(0, ROPE, 2, dtype=F32) / ROPE))
    invf = jnp.tile(inv_freq, 4).reshape(1, LANES)

    assert norm_mix_g.shape[0] == 1, "single-layer block only"
    wi = w_in[0]
    kr_cols = wi[:, Q_LORA + KV_LORA:Q_LORA + KV_LORA + ROPE]
    w_in_l = jnp.concatenate(
        [wi[:, :Q_LORA + KV_LORA], wi[:, Q_LORA + KV_LORA + ROPE:], kr_cols,
         _half_swap(kr_cols)], axis=1).astype(BF16)
    wq = w_q_up[0].reshape(Q_LORA, HEADS, QK)
    wq_pe = wq[:, :, NOPE:]
    w_q_l = jnp.concatenate(
        [wq[:, :, :NOPE].reshape(Q_LORA, HEADS * NOPE),
         jnp.concatenate([wq_pe, _half_swap(wq_pe)], axis=-1).reshape(Q_LORA, HEADS * LANES)],
        axis=1).astype(BF16)
    wkv = w_kv_up[0].reshape(KV_LORA, HEADS, NOPE + VDIM)
    w_k_l = wkv[:, :, :NOPE].reshape(KV_LORA, HEADS * NOPE).astype(BF16)
    w_vt_l = wkv[:, :, NOPE:].reshape(KV_LORA, HEADS * VDIM).T.astype(BF16)

    q, k, vt, p = _front(
        x2, pos2, invf, norm_mix_g[0].reshape(1, D_MODEL), w_in_l,
        q_norm_g[0].reshape(1, Q_LORA), w_q_l, kv_norm_g[0].reshape(1, KV_LORA), w_k_l, w_vt_l,
        pool_w[0].astype(BF16), pool_scale[0].reshape(1, POOL_W), seq=seq, tm=FRONT_TM,
        sub=FRONT_SUB)
    a = _attn(q, k, vt, batch=batch, seq=seq, tq=ATTN_TQ)
    out = _back(
        x2, a, p, w_out[0, :ATTN_W].astype(BF16), w_out[0, ATTN_W:].astype(BF16),
        norm_mlp_g[0].reshape(1, D_MODEL), w_mlp_up[0].astype(BF16),
        w_mlp_down[0].astype(BF16), norm_final_g.reshape(1, D_MODEL), tm=BACK_TM, fc=BACK_FC)
    return out.reshape(batch, seq, D_MODEL)
```

```python
import functools
import math

import jax
import jax.numpy as jnp
from jax import lax
from jax.experimental import pallas as pl
from jax.experimental.pallas import tpu as pltpu

D_MODEL = 1024
HEADS = 4
NOPE = 128
ROPE = 64
VDIM = 128
QK = NOPE + ROPE
ATTN_W = HEADS * VDIM
Q_LORA = 384
KV_LORA = 256
GROUPS = 4
WINDOWS = (2, 4, 8, 16)
POOL_W = 512
POOL_CH = 128
D_FF = 4096
ROPE_THETA = 10000.0
EPS = 1e-6
HALO = 16
LANES = 128
LOOKAHEAD = 3
VMEM_LIMIT = 56 * 1024 * 1024
FRONT_TM = 512
FRONT_SUB = 256
ATTN_TQ = 512
BACK_TM = 512
BACK_SUB = 256
BACK_FC = 1024

F32 = jnp.float32
BF16 = jnp.bfloat16


def _rmsnorm(x, g):
    ms = jnp.mean(x * x, axis=-1, keepdims=True)
    return x * lax.rsqrt(ms + EPS) * g


def _front_kernel(x_ref, pos_ref, invf_ref, g_mix_ref, w_in_ref, g_q_ref, w_q_ref, g_kv_ref,
                  w_k_ref, w_vt_ref, pool_w_ref, pool_scale_ref,
                  q_ref, k_ref, vt_ref, p_ref, halo_ref, tab_ref, *, tm, sub, tiles_per_seq,
                  q_scale):
    i = pl.program_id(0)
    tile_in_seq = i % tiles_per_seq
    rows = [pl.ds(r * sub, sub) for r in range(tm // sub)]
    lane = lax.broadcasted_iota(jnp.int32, (1, LANES), 1)
    u_off = Q_LORA + KV_LORA

    @pl.when(tile_in_seq == 0)
    def _():
        halo_ref[...] = jnp.zeros_like(halo_ref)

    quarter = tm // 4
    ang = pos_ref[...].astype(F32) * invf_ref[...]
    cos4 = jnp.cos(ang)
    sin4 = jnp.sin(ang)
    for k in range(4):
        def group_to(x, dst, k=k):
            shift = (32 * (dst - k)) % LANES
            return pltpu.roll(x, shift, 1) if shift else x
        cos_k = jnp.where(lane < 32, group_to(cos4, 0), group_to(cos4, 1))
        sin_k = jnp.where(lane < 96, -group_to(sin4, 2), group_to(sin4, 3))
        tab_ref[k * quarter:(k + 1) * quarter, :] = jnp.where(lane < 64, cos_k, sin_k)

    n = [_rmsnorm(x_ref[r, :], g_mix_ref[...]).astype(BF16) for r in rows]
    proj = [jnp.dot(ns, w_in_ref[...], preferred_element_type=F32) for ns in n]
    cq = [_rmsnorm(pr[:, :Q_LORA], g_q_ref[...]).astype(BF16) for pr in proj]
    ckv = [_rmsnorm(pr[:, Q_LORA:u_off], g_kv_ref[...]).astype(BF16) for pr in proj]
    u = [pr[:, u_off:u_off + POOL_W] for pr in proj]

    def rope(pair, tab):
        t = pair * tab
        return (t + pltpu.roll(t, 64, 1))[:, :ROPE]

    for r, pr, cqs, ckvs in zip(rows, proj, cq, ckv):
        tab = tab_ref[r, :]
        q = jnp.dot(cqs, w_q_ref[...], preferred_element_type=F32)
        kn = jnp.dot(ckvs, w_k_ref[...], preferred_element_type=F32)
        vt_ref[:, r] = lax.dot_general(w_vt_ref[...], ckvs, (((1,), (1,)), ((), ())),
                                       preferred_element_type=F32).astype(BF16)
        k_pe = rope(pr[:, u_off + POOL_W:], tab).astype(BF16)
        for h in range(HEADS):
            q_ref[h, r, :NOPE] = (q[:, h * NOPE:(h + 1) * NOPE] * q_scale).astype(BF16)
            pe = rope(q[:, HEADS * NOPE + h * LANES:HEADS * NOPE + (h + 1) * LANES], tab)
            q_ref[h, r, NOPE:] = (pe * q_scale).astype(BF16)
            k_ref[h, r, :NOPE] = kn[:, h * NOPE:(h + 1) * NOPE].astype(BF16)
            k_ref[h, r, NOPE:] = k_pe

    prev = halo_ref[...]
    halo_ref[...] = u[-1][sub - HALO:, :]
    for si, (r, us) in enumerate(zip(rows, u)):
        t_seq = (tile_in_seq * tm + si * sub) + lax.broadcasted_iota(jnp.int32, (sub, 1), 0)
        for g, w in enumerate(WINDOWS):
            cols = slice(g * POOL_CH, (g + 1) * POOL_CH)
            ug = us[:, cols]
            s = jnp.concatenate([prev[:, cols], ug], axis=0)
            for step in range(g + 1):
                s = s + pltpu.roll(s, 1 << step, 0)
            inv_cnt = 1.0 / jnp.minimum(t_seq + 1, w).astype(F32)
            pooled = (s[HALO:, :] * inv_cnt - ug).astype(BF16)
            y = jnp.dot(pooled, pool_w_ref[g], preferred_element_type=F32)
            p_ref[r, cols] = (y * pool_scale_ref[:, cols]).astype(BF16)
        prev = us[sub - HALO:, :]


def _const_spec(shape):
    nd = len(shape)
    return pl.BlockSpec(shape, lambda *_: (0,) * nd, pipeline_mode=pl.Buffered(1))


def _front(x2, pos2, invf, g_mix, w_in, g_q, w_q, g_kv, w_k, w_vt, pool_w, pool_scale, *, seq, tm,
           sub):
    t = x2.shape[0]
    tiles_per_seq = seq // tm
    q_scale = (QK ** -0.5) * math.log2(math.e)
    kern = functools.partial(_front_kernel, tm=tm, sub=sub, tiles_per_seq=tiles_per_seq,
                             q_scale=q_scale)
    return pl.pallas_call(
        kern,
        grid=(t // tm,),
        in_specs=[
            pl.BlockSpec((tm, D_MODEL), lambda i: (i, 0)),
            pl.BlockSpec((tm // 4, LANES), lambda i: (i, 0)),
            _const_spec(invf.shape), _const_spec(g_mix.shape), _const_spec(w_in.shape),
            _const_spec(g_q.shape), _const_spec(w_q.shape), _const_spec(g_kv.shape),
            _const_spec(w_k.shape), _const_spec(w_vt.shape), _const_spec(pool_w.shape),
            _const_spec(pool_scale.shape),
        ],
        out_specs=[
            pl.BlockSpec((HEADS, tm, QK), lambda i: (0, i, 0)),
            pl.BlockSpec((HEADS, tm, QK), lambda i: (0, i, 0)),
            pl.BlockSpec((ATTN_W, tm), lambda i: (0, i)),
            pl.BlockSpec((tm, POOL_W), lambda i: (i, 0)),
        ],
        out_shape=[
            jax.ShapeDtypeStruct((HEADS, t, QK), BF16),
            jax.ShapeDtypeStruct((HEADS, t, QK), BF16),
            jax.ShapeDtypeStruct((ATTN_W, t), BF16),
            jax.ShapeDtypeStruct((t, POOL_W), BF16),
        ],
        scratch_shapes=[pltpu.VMEM((HALO, POOL_W), F32), pltpu.VMEM((tm, LANES), F32)],
        compiler_params=pltpu.CompilerParams(
            dimension_semantics=("arbitrary",), vmem_limit_bytes=VMEM_LIMIT),
        name="front",
    )(x2, pos2, invf, g_mix, w_in, g_q, w_q, g_kv, w_k, w_vt, pool_w, pool_scale)


def _attn_kernel(q_ref, k_ref, vt_ref, o_ref, acc_ref, *, tq, tk, nq):
    i = pl.program_id(1)
    half = tk // 2
    tri = (lax.broadcasted_iota(jnp.int32, (half, half), 1)
           >= lax.broadcasted_iota(jnp.int32, (half, half), 0))

    def scores(item):
        h, k0, rows, q0, cols, _ = item
        return lax.dot_general(k_ref[h, pl.ds(k0, rows), :], q_ref[h, q0:q0 + cols, :],
                               (((1,), (1,)), ((), ())), preferred_element_type=F32)

    def run(items, state, first=False):
        state = list(state)
        pending = {n: scores(items[n]) for n in range(min(LOOKAHEAD, len(items)))}
        for n, item in enumerate(items):
            if n + LOOKAHEAD < len(items):
                pending[n + LOOKAHEAD] = scores(items[n + LOOKAHEAD])
            h, k0, rows, q0, cols, masked = item
            s = pending.pop(n)
            if masked:
                lower = jnp.where(tri, s[:, :half], -jnp.inf)
                s = lower if cols == half else jnp.concatenate([lower, s[:, half:]], axis=1)
            vt = vt_ref[h * VDIM:(h + 1) * VDIM, pl.ds(k0, rows)]
            if first and state[h] is None:
                assert cols == tq
                m = jnp.max(s, axis=0, keepdims=True)
                p = jnp.exp2(s - m)
                state[h] = (m, jnp.sum(p, axis=0, keepdims=True))
                acc_ref[h] = jnp.dot(vt, p.astype(BF16), preferred_element_type=F32)
                continue
            m_all, l_all = state[h]
            m_old, l_old = m_all[:, q0:q0 + cols], l_all[:, q0:q0 + cols]
            m = jnp.maximum(m_old, jnp.max(s, axis=0, keepdims=True))
            alpha = jnp.exp2(m_old - m)
            p = jnp.exp2(s - m)
            l = alpha * l_old + jnp.sum(p, axis=0, keepdims=True)
            acc_ref[h, :, q0:q0 + cols] = (
                alpha * acc_ref[h, :, q0:q0 + cols]
                + jnp.dot(vt, p.astype(BF16), preferred_element_type=F32))
            if cols != tq:
                assert q0 + cols == tq
                m = jnp.concatenate([m_all[:, :q0], m], axis=1)
                l = jnp.concatenate([l_all[:, :q0], l], axis=1)
            state[h] = (m, l)
        return tuple(state)

    for qb in range(nq):
        @pl.when(i == qb)
        def _(qb=qb):
            items = ([(h, qb * tk, half, 0, tq, True) for h in range(HEADS)]
                     + [(h, qb * tk + half, half, half, tq - half, True) for h in range(HEADS)]
                     + [(h, j * tk, tk, 0, tq, False) for j in range(qb) for h in range(HEADS)])
            state = run(items, (None,) * HEADS, first=True)
            for h in range(HEADS):
                o_ref[:, h * VDIM:(h + 1) * VDIM] = (acc_ref[h] / state[h][1]).T.astype(BF16)


def _attn(q, k, vt, *, batch, seq, tq):
    t = vt.shape[1]
    nq = seq // tq
    kern = functools.partial(_attn_kernel, tq=tq, tk=tq, nq=nq)
    return pl.pallas_call(
        kern,
        grid=(batch, nq),
        in_specs=[
            pl.BlockSpec((HEADS, tq, QK), lambda b, i: (0, b * nq + i, 0)),
            pl.BlockSpec((HEADS, seq, QK), lambda b, i: (0, b, 0)),
            pl.BlockSpec((ATTN_W, seq), lambda b, i: (0, b)),
        ],
        out_specs=pl.BlockSpec((tq, ATTN_W), lambda b, i: (b * nq + i, 0)),
        out_shape=jax.ShapeDtypeStruct((t, ATTN_W), BF16),
        scratch_shapes=[pltpu.VMEM((HEADS, VDIM, tq), F32)],
        compiler_params=pltpu.CompilerParams(
            dimension_semantics=("arbitrary", "arbitrary"), vmem_limit_bytes=VMEM_LIMIT),
        name="attn",
    )(q, k, vt)


def _back_kernel(x_ref, a_ref, p_ref, w_oa_ref, w_op_ref, g_mlp_ref, w_up_ref, w_down_ref,
                 g_fin_ref, o_ref, *, tm, sub, fc):
    rows = [pl.ds(r * sub, sub) for r in range(tm // sub)]
    h = [x_ref[r, :]
         + jnp.dot(a_ref[r, :], w_oa_ref[...], preferred_element_type=F32)
         + jnp.dot(p_ref[r, :], w_op_ref[...], preferred_element_type=F32) for r in rows]
    n = [_rmsnorm(hs, g_mlp_ref[...]).astype(BF16) for hs in h]
    acc = [None] * len(rows)
    for c in range(D_FF // fc):
        for si, ns in enumerate(n):
            m = jnp.dot(ns, w_up_ref[:, c * fc:(c + 1) * fc], preferred_element_type=F32)
            r = jnp.square(jnp.maximum(m, 0.0)).astype(BF16)
            d = jnp.dot(r, w_down_ref[c * fc:(c + 1) * fc, :], preferred_element_type=F32)
            acc[si] = d if acc[si] is None else acc[si] + d
    for r, hs, accs in zip(rows, h, acc):
        o_ref[r, :] = _rmsnorm(hs + accs, g_fin_ref[...])


def _back(x2, a, p, w_oa, w_op, g_mlp, w_up, w_down, g_fin, *, tm, sub, fc):
    t = x2.shape[0]
    kern = functools.partial(_back_kernel, tm=tm, sub=sub, fc=fc)
    return pl.pallas_call(
        kern,
        grid=(t // tm,),
        in_specs=[
            pl.BlockSpec((tm, D_MODEL), lambda i: (i, 0)),
            pl.BlockSpec((tm, ATTN_W), lambda i: (i, 0)),
            pl.BlockSpec((tm, POOL_W), lambda i: (i, 0)),
            _const_spec(w_oa.shape), _const_spec(w_op.shape), _const_spec(g_mlp.shape),
            _const_spec(w_up.shape), _const_spec(w_down.shape), _const_spec(g_fin.shape),
        ],
        out_specs=pl.BlockSpec((tm, D_MODEL), lambda i: (i, 0)),
        out_shape=jax.ShapeDtypeStruct((t, D_MODEL), F32),
        compiler_params=pltpu.CompilerParams(
            dimension_semantics=("arbitrary",), vmem_limit_bytes=VMEM_LIMIT),
        name="back",
    )(x2, a, p, w_oa, w_op, g_mlp, w_up, w_down, g_fin)


def _half_swap(w):
    half = w.shape[-1] // 2
    return jnp.concatenate([w[..., half:], w[..., :half]], axis=-1)


def kernel(x, positions, norm_mix_g, w_in, q_norm_g, w_q_up, kv_norm_g, w_kv_up, pool_w,
           pool_scale, w_out, norm_mlp_g, w_mlp_up, w_mlp_down, norm_final_g):
    batch, seq, _ = x.shape
    t = batch * seq
    x2 = x.reshape(t, D_MODEL)
    pos2 = jnp.broadcast_to(
        positions.reshape(t // FRONT_TM, 4, FRONT_TM // 4, 1).transpose(0, 2, 1, 3),
        (t // FRONT_TM, FRONT_TM // 4, 4, LANES // 4)).reshape(t // 4, LANES)
    inv_freq = 1.0 / (ROPE_THETA ** (jnp.arange(0, ROPE, 2, dtype=F32) / ROPE))
    invf = jnp.tile(inv_freq, 4).reshape(1, LANES)

    assert norm_mix_g.shape[0] == 1, "single-layer block only"
    wi = w_in[0]
    kr_cols = wi[:, Q_LORA + KV_LORA:Q_LORA + KV_LORA + ROPE]
    w_in_l = jnp.concatenate(
        [wi[:, :Q_LORA + KV_LORA], wi[:, Q_LORA + KV_LORA + ROPE:], kr_cols,
         _half_swap(kr_cols)], axis=1).astype(BF16)
    wq = w_q_up[0].reshape(Q_LORA, HEADS, QK)
    wq_pe = wq[:, :, NOPE:]
    w_q_l = jnp.concatenate(
        [wq[:, :, :NOPE].reshape(Q_LORA, HEADS * NOPE),
         jnp.concatenate([wq_pe, _half_swap(wq_pe)], axis=-1).reshape(Q_LORA, HEADS * LANES)],
        axis=1).astype(BF16)
    wkv = w_kv_up[0].reshape(KV_LORA, HEADS, NOPE + VDIM)
    w_k_l = wkv[:, :, :NOPE].reshape(KV_LORA, HEADS * NOPE).astype(BF16)
    w_vt_l = wkv[:, :, NOPE:].reshape(KV_LORA, HEADS * VDIM).T.astype(BF16)

    q, k, vt, p = _front(
        x2, pos2, invf, norm_mix_g[0].reshape(1, D_MODEL), w_in_l,
        q_norm_g[0].reshape(1, Q_LORA), w_q_l, kv_norm_g[0].reshape(1, KV_LORA), w_k_l, w_vt_l,
        pool_w[0].astype(BF16), pool_scale[0].reshape(1, POOL_W), seq=seq, tm=FRONT_TM,
        sub=FRONT_SUB)
    a = _attn(q, k, vt, batch=batch, seq=seq, tq=ATTN_TQ)
    out = _back(
        x2, a, p, w_out[0, :ATTN_W].astype(BF16), w_out[0, ATTN_W:].astype(BF16),
        norm_mlp_g[0].reshape(1, D_MODEL), w_mlp_up[0].astype(BF16),
        w_mlp_down[0].astype(BF16), norm_final_g.reshape(1, D_MODEL), tm=BACK_TM,
        sub=BACK_SUB, fc=BACK_FC)
    return out.reshape(batch, seq, D_MODEL)
```

```python
import functools
import math

import jax
import jax.numpy as jnp
from jax import lax
from jax.experimental import pallas as pl
from jax.experimental.pallas import tpu as pltpu

D_MODEL = 1024
HEADS = 4
NOPE = 128
ROPE = 64
VDIM = 128
QK = NOPE + ROPE
ATTN_W = HEADS * VDIM
Q_LORA = 384
KV_LORA = 256
GROUPS = 4
WINDOWS = (2, 4, 8, 16)
POOL_W = 512
POOL_CH = 128
D_FF = 4096
ROPE_THETA = 10000.0
EPS = 1e-6
HALO = 16
LANES = 128
LOOKAHEAD = 3
VMEM_LIMIT = 56 * 1024 * 1024
FRONT_TM = 1024
FRONT_SUB = 256
ATTN_TQ = 512
BACK_TM = 1024
BACK_SUB = 256
BACK_FC = 1024

F32 = jnp.float32
BF16 = jnp.bfloat16


def _rmsnorm(x, g):
    ms = jnp.mean(x * x, axis=-1, keepdims=True)
    return x * lax.rsqrt(ms + EPS) * g


def _front_kernel(x_ref, pos_ref, invf_ref, g_mix_ref, w_in_ref, g_q_ref, w_q_ref, g_kv_ref,
                  w_k_ref, w_vt_ref, pool_w_ref, pool_scale_ref,
                  q_ref, k_ref, vt_ref, p_ref, halo_ref, tab_ref, *, tm, sub, tiles_per_seq,
                  q_scale):
    i = pl.program_id(0)
    tile_in_seq = i % tiles_per_seq
    rows = [pl.ds(r * sub, sub) for r in range(tm // sub)]
    lane = lax.broadcasted_iota(jnp.int32, (1, LANES), 1)
    u_off = Q_LORA + KV_LORA

    @pl.when(tile_in_seq == 0)
    def _():
        halo_ref[...] = jnp.zeros_like(halo_ref)

    quarter = tm // 4
    ang = pos_ref[...].astype(F32) * invf_ref[...]
    cos4 = jnp.cos(ang)
    sin4 = jnp.sin(ang)
    for k in range(4):
        def group_to(x, dst, k=k):
            shift = (32 * (dst - k)) % LANES
            return pltpu.roll(x, shift, 1) if shift else x
        cos_k = jnp.where(lane < 32, group_to(cos4, 0), group_to(cos4, 1))
        sin_k = jnp.where(lane < 96, -group_to(sin4, 2), group_to(sin4, 3))
        tab_ref[k * quarter:(k + 1) * quarter, :] = jnp.where(lane < 64, cos_k, sin_k)

    n = [_rmsnorm(x_ref[r, :], g_mix_ref[...]).astype(BF16) for r in rows]
    proj = [jnp.dot(ns, w_in_ref[...], preferred_element_type=F32) for ns in n]
    cq = [_rmsnorm(pr[:, :Q_LORA], g_q_ref[...]).astype(BF16) for pr in proj]
    ckv = [_rmsnorm(pr[:, Q_LORA:u_off], g_kv_ref[...]).astype(BF16) for pr in proj]
    u = [pr[:, u_off:u_off + POOL_W] for pr in proj]

    def rope(pair, tab):
        t = pair * tab
        return (t + pltpu.roll(t, 64, 1))[:, :ROPE]

    for r, pr, cqs, ckvs in zip(rows, proj, cq, ckv):
        tab = tab_ref[r, :]
        q = jnp.dot(cqs, w_q_ref[...], preferred_element_type=F32)
        kn = jnp.dot(ckvs, w_k_ref[...], preferred_element_type=F32)
        vt_ref[:, r] = lax.dot_general(w_vt_ref[...], ckvs, (((1,), (1,)), ((), ())),
                                       preferred_element_type=F32).astype(BF16)
        k_pe = rope(pr[:, u_off + POOL_W:], tab).astype(BF16)
        for h in range(HEADS):
            q_ref[h, r, :NOPE] = (q[:, h * NOPE:(h + 1) * NOPE] * q_scale).astype(BF16)
            pe = rope(q[:, HEADS * NOPE + h * LANES:HEADS * NOPE + (h + 1) * LANES], tab)
            q_ref[h, r, NOPE:] = (pe * q_scale).astype(BF16)
            k_ref[h, r, :NOPE] = kn[:, h * NOPE:(h + 1) * NOPE].astype(BF16)
            k_ref[h, r, NOPE:] = k_pe

    prev = halo_ref[...]
    halo_ref[...] = u[-1][sub - HALO:, :]
    for si, (r, us) in enumerate(zip(rows, u)):
        t_seq = (tile_in_seq * tm + si * sub) + lax.broadcasted_iota(jnp.int32, (sub, 1), 0)
        for g, w in enumerate(WINDOWS):
            cols = slice(g * POOL_CH, (g + 1) * POOL_CH)
            ug = us[:, cols]
            s = jnp.concatenate([prev[:, cols], ug], axis=0)
            for step in range(g + 1):
                s = s + pltpu.roll(s, 1 << step, 0)
            inv_cnt = 1.0 / jnp.minimum(t_seq + 1, w).astype(F32)
            pooled = (s[HALO:, :] * inv_cnt - ug).astype(BF16)
            y = jnp.dot(pooled, pool_w_ref[g], preferred_element_type=F32)
            p_ref[r, cols] = (y * pool_scale_ref[:, cols]).astype(BF16)
        prev = us[sub - HALO:, :]


def _const_spec(shape):
    nd = len(shape)
    return pl.BlockSpec(shape, lambda *_: (0,) * nd, pipeline_mode=pl.Buffered(1))


def _front(x2, pos2, invf, g_mix, w_in, g_q, w_q, g_kv, w_k, w_vt, pool_w, pool_scale, *, seq, tm,
           sub):
    t = x2.shape[0]
    tiles_per_seq = seq // tm
    q_scale = (QK ** -0.5) * math.log2(math.e)
    kern = functools.partial(_front_kernel, tm=tm, sub=sub, tiles_per_seq=tiles_per_seq,
                             q_scale=q_scale)
    return pl.pallas_call(
        kern,
        grid=(t // tm,),
        in_specs=[
            pl.BlockSpec((tm, D_MODEL), lambda i: (i, 0)),
            pl.BlockSpec((tm // 4, LANES), lambda i: (i, 0)),
            _const_spec(invf.shape), _const_spec(g_mix.shape), _const_spec(w_in.shape),
            _const_spec(g_q.shape), _const_spec(w_q.shape), _const_spec(g_kv.shape),
            _const_spec(w_k.shape), _const_spec(w_vt.shape), _const_spec(pool_w.shape),
            _const_spec(pool_scale.shape),
        ],
        out_specs=[
            pl.BlockSpec((HEADS, tm, QK), lambda i: (0, i, 0)),
            pl.BlockSpec((HEADS, tm, QK), lambda i: (0, i, 0)),
            pl.BlockSpec((ATTN_W, tm), lambda i: (0, i)),
            pl.BlockSpec((tm, POOL_W), lambda i: (i, 0)),
        ],
        out_shape=[
            jax.ShapeDtypeStruct((HEADS, t, QK), BF16),
            jax.ShapeDtypeStruct((HEADS, t, QK), BF16),
            jax.ShapeDtypeStruct((ATTN_W, t), BF16),
            jax.ShapeDtypeStruct((t, POOL_W), BF16),
        ],
        scratch_shapes=[pltpu.VMEM((HALO, POOL_W), F32), pltpu.VMEM((tm, LANES), F32)],
        compiler_params=pltpu.CompilerParams(
            dimension_semantics=("arbitrary",), vmem_limit_bytes=VMEM_LIMIT),
        name="front",
    )(x2, pos2, invf, g_mix, w_in, g_q, w_q, g_kv, w_k, w_vt, pool_w, pool_scale)


def _attn_kernel(q_ref, k_ref, vt_ref, o_ref, acc_ref, *, tq, tk, nq):
    i = pl.program_id(1)
    half = tk // 2
    tri = (lax.broadcasted_iota(jnp.int32, (half, half), 1)
           >= lax.broadcasted_iota(jnp.int32, (half, half), 0))

    def scores(item):
        h, k0, rows, q0, cols, _ = item
        return lax.dot_general(k_ref[h, pl.ds(k0, rows), :], q_ref[h, q0:q0 + cols, :],
                               (((1,), (1,)), ((), ())), preferred_element_type=F32)

    def run(items, state, first=False):
        state = list(state)
        pending = {n: scores(items[n]) for n in range(min(LOOKAHEAD, len(items)))}
        for n, item in enumerate(items):
            if n + LOOKAHEAD < len(items):
                pending[n + LOOKAHEAD] = scores(items[n + LOOKAHEAD])
            h, k0, rows, q0, cols, masked = item
            s = pending.pop(n)
            if masked:
                lower = jnp.where(tri, s[:, :half], -jnp.inf)
                s = lower if cols == half else jnp.concatenate([lower, s[:, half:]], axis=1)
            vt = vt_ref[h * VDIM:(h + 1) * VDIM, pl.ds(k0, rows)]
            if first and state[h] is None:
                assert cols == tq
                m = jnp.max(s, axis=0, keepdims=True)
                p = jnp.exp2(s - m)
                state[h] = (m, jnp.sum(p, axis=0, keepdims=True))
                acc_ref[h] = jnp.dot(vt, p.astype(BF16), preferred_element_type=F32)
                continue
            m_all, l_all = state[h]
            m_old, l_old = m_all[:, q0:q0 + cols], l_all[:, q0:q0 + cols]
            m = jnp.maximum(m_old, jnp.max(s, axis=0, keepdims=True))
            alpha = jnp.exp2(m_old - m)
            p = jnp.exp2(s - m)
            l = alpha * l_old + jnp.sum(p, axis=0, keepdims=True)
            acc_ref[h, :, q0:q0 + cols] = (
                alpha * acc_ref[h, :, q0:q0 + cols]
                + jnp.dot(vt, p.astype(BF16), preferred_element_type=F32))
            if cols != tq:
                assert q0 + cols == tq
                m = jnp.concatenate([m_all[:, :q0], m], axis=1)
                l = jnp.concatenate([l_all[:, :q0], l], axis=1)
            state[h] = (m, l)
        return tuple(state)

    for qb in range(nq):
        @pl.when(i == qb)
        def _(qb=qb):
            items = ([(h, qb * tk, half, 0, tq, True) for h in range(HEADS)]
                     + [(h, qb * tk + half, half, half, tq - half, True) for h in range(HEADS)]
                     + [(h, j * tk, tk, 0, tq, False) for j in range(qb) for h in range(HEADS)])
            state = run(items, (None,) * HEADS, first=True)
            for h in range(HEADS):
                o_ref[:, h * VDIM:(h + 1) * VDIM] = (acc_ref[h] / state[h][1]).T.astype(BF16)


def _attn(q, k, vt, *, batch, seq, tq):
    t = vt.shape[1]
    nq = seq // tq
    kern = functools.partial(_attn_kernel, tq=tq, tk=tq, nq=nq)
    return pl.pallas_call(
        kern,
        grid=(batch, nq),
        in_specs=[
            pl.BlockSpec((HEADS, tq, QK), lambda b, i: (0, b * nq + i, 0)),
            pl.BlockSpec((HEADS, seq, QK), lambda b, i: (0, b, 0)),
            pl.BlockSpec((ATTN_W, seq), lambda b, i: (0, b)),
        ],
        out_specs=pl.BlockSpec((tq, ATTN_W), lambda b, i: (b * nq + i, 0)),
        out_shape=jax.ShapeDtypeStruct((t, ATTN_W), BF16),
        scratch_shapes=[pltpu.VMEM((HEADS, VDIM, tq), F32)],
        compiler_params=pltpu.CompilerParams(
            dimension_semantics=("arbitrary", "arbitrary"), vmem_limit_bytes=VMEM_LIMIT),
        name="attn",
    )(q, k, vt)


def _back_kernel(x_ref, a_ref, p_ref, w_oa_ref, w_op_ref, g_mlp_ref, w_up_ref, w_down_ref,
                 g_fin_ref, o_ref, *, tm, sub, fc):
    rows = [pl.ds(r * sub, sub) for r in range(tm // sub)]
    h = [x_ref[r, :]
         + jnp.dot(a_ref[r, :], w_oa_ref[...], preferred_element_type=F32)
         + jnp.dot(p_ref[r, :], w_op_ref[...], preferred_element_type=F32) for r in rows]
    n = [_rmsnorm(hs, g_mlp_ref[...]).astype(BF16) for hs in h]
    acc = [None] * len(rows)
    for c in range(D_FF // fc):
        for si, ns in enumerate(n):
            m = jnp.dot(ns, w_up_ref[:, c * fc:(c + 1) * fc], preferred_element_type=F32)
            r = jnp.square(jnp.maximum(m, 0.0)).astype(BF16)
            d = jnp.dot(r, w_down_ref[c * fc:(c + 1) * fc, :], preferred_element_type=F32)
            acc[si] = d if acc[si] is None else acc[si] + d
    for r, hs, accs in zip(rows, h, acc):
        o_ref[r, :] = _rmsnorm(hs + accs, g_fin_ref[...])


def _back(x2, a, p, w_oa, w_op, g_mlp, w_up, w_down, g_fin, *, tm, sub, fc):
    t = x2.shape[0]
    kern = functools.partial(_back_kernel, tm=tm, sub=sub, fc=fc)
    return pl.pallas_call(
        kern,
        grid=(t // tm,),
        in_specs=[
            pl.BlockSpec((tm, D_MODEL), lambda i: (i, 0)),
            pl.BlockSpec((tm, ATTN_W), lambda i: (i, 0)),
            pl.BlockSpec((tm, POOL_W), lambda i: (i, 0)),
            _const_spec(w_oa.shape), _const_spec(w_op.shape), _const_spec(g_mlp.shape),
            _const_spec(w_up.shape), _const_spec(w_down.shape), _const_spec(g_fin.shape),
        ],
        out_specs=pl.BlockSpec((tm, D_MODEL), lambda i: (i, 0)),
        out_shape=jax.ShapeDtypeStruct((t, D_MODEL), F32),
        compiler_params=pltpu.CompilerParams(
            dimension_semantics=("arbitrary",), vmem_limit_bytes=VMEM_LIMIT),
        name="back",
    )(x2, a, p, w_oa, w_op, g_mlp, w_up, w_down, g_fin)


def _half_swap(w):
    half = w.shape[-1] // 2
    return jnp.concatenate([w[..., half:], w[..., :half]], axis=-1)


def kernel(x, positions, norm_mix_g, w_in, q_norm_g, w_q_up, kv_norm_g, w_kv_up, pool_w,
           pool_scale, w_out, norm_mlp_g, w_mlp_up, w_mlp_down, norm_final_g):
    batch, seq, _ = x.shape
    t = batch * seq
    x2 = x.reshape(t, D_MODEL)
    pos2 = jnp.broadcast_to(
        positions.reshape(t // FRONT_TM, 4, FRONT_TM // 4, 1).transpose(0, 2, 1, 3),
        (t // FRONT_TM, FRONT_TM // 4, 4, LANES // 4)).reshape(t // 4, LANES)
    inv_freq = 1.0 / (ROPE_THETA ** (jnp.arange(0, ROPE, 2, dtype=F32) / ROPE))
    invf = jnp.tile(inv_freq, 4).reshape(1, LANES)

    assert norm_mix_g.shape[0] == 1, "single-layer block only"
    wi = w_in[0]
    kr_cols = wi[:, Q_LORA + KV_LORA:Q_LORA + KV_LORA + ROPE]
    w_in_l = jnp.concatenate(
        [wi[:, :Q_LORA + KV_LORA], wi[:, Q_LORA + KV_LORA + ROPE:], kr_cols,
         _half_swap(kr_cols)], axis=1).astype(BF16)
    wq = w_q_up[0].reshape(Q_LORA, HEADS, QK)
    wq_pe = wq[:, :, NOPE:]
    w_q_l = jnp.concatenate(
        [wq[:, :, :NOPE].reshape(Q_LORA, HEADS * NOPE),
         jnp.concatenate([wq_pe, _half_swap(wq_pe)], axis=-1).reshape(Q_LORA, HEADS * LANES)],
        axis=1).astype(BF16)
    wkv = w_kv_up[0].reshape(KV_LORA, HEADS, NOPE + VDIM)
    w_k_l = wkv[:, :, :NOPE].reshape(KV_LORA, HEADS * NOPE).astype(BF16)
    w_vt_l = wkv[:, :, NOPE:].reshape(KV_LORA, HEADS * VDIM).T.astype(BF16)

    q, k, vt, p = _front(
        x2, pos2, invf, norm_mix_g[0].reshape(1, D_MODEL), w_in_l,
        q_norm_g[0].reshape(1, Q_LORA), w_q_l, kv_norm_g[0].reshape(1, KV_LORA), w_k_l, w_vt_l,
        pool_w[0].astype(BF16), pool_scale[0].reshape(1, POOL_W), seq=seq, tm=FRONT_TM,
        sub=FRONT_SUB)
    a = _attn(q, k, vt, batch=batch, seq=seq, tq=ATTN_TQ)
    out = _back(
        x2, a, p, w_out[0, :ATTN_W].astype(BF16), w_out[0, ATTN_W:].astype(BF16),
        norm_mlp_g[0].reshape(1, D_MODEL), w_mlp_up[0].astype(BF16),
        w_mlp_down[0].astype(BF16), norm_final_g.reshape(1, D_MODEL), tm=BACK_TM,
        sub=BACK_SUB, fc=BACK_FC)
    return out.reshape(batch, seq, D_MODEL)
```

```python
import functools
import math

import jax
import jax.numpy as jnp
from jax import lax
from jax.experimental import pallas as pl
from jax.experimental.pallas import tpu as pltpu

D_MODEL = 1024
HEADS = 4
NOPE = 128
ROPE = 64
VDIM = 128
QK = NOPE + ROPE
ATTN_W = HEADS * VDIM
Q_LORA = 384
KV_LORA = 256
GROUPS = 4
WINDOWS = (2, 4, 8, 16)
POOL_W = 512
POOL_CH = 128
D_FF = 4096
ROPE_THETA = 10000.0
EPS = 1e-6
HALO = 16
LANES = 128
LOOKAHEAD = 3
MIB = 1024 * 1024
FRONT_VMEM = 34 * MIB
ATTN_VMEM = 24 * MIB
BACK_VMEM = 56 * MIB
FRONT_TM = 1024
FRONT_SUB = 256
ATTN_TQ = 512
BACK_TM = 1024
BACK_SUB = 256
BACK_FC = 1024

F32 = jnp.float32
BF16 = jnp.bfloat16


def _rmsnorm(x, g):
    ms = jnp.mean(x * x, axis=-1, keepdims=True)
    return x * lax.rsqrt(ms + EPS) * g


def _front_kernel(x_ref, pos_ref, invf_ref, g_mix_ref, w_a_ref, w_u_ref, g_q_ref, w_q_ref, g_kv_ref,
                  w_k_ref, w_vt_ref,
                  q_ref, k_ref, vt_ref, p_ref, halo_ref, tab_ref, *, tm, sub, tiles_per_seq,
                  q_scale):
    i = pl.program_id(0)
    tile_in_seq = i % tiles_per_seq
    rows = [pl.ds(r * sub, sub) for r in range(tm // sub)]
    lane = lax.broadcasted_iota(jnp.int32, (1, LANES), 1)
    kr_off = Q_LORA + KV_LORA

    @pl.when(tile_in_seq == 0)
    def _():
        halo_ref[...] = jnp.zeros_like(halo_ref)

    quarter = tm // 4
    pos = pos_ref[...]
    pos4 = jnp.where(lane < 64, jnp.where(lane < 32, pos[:, 0:1], pos[:, 1:2]),
                     jnp.where(lane < 96, pos[:, 2:3], pos[:, 3:4]))
    ang = pos4.astype(F32) * invf_ref[...]
    cos4 = jnp.cos(ang)
    sin4 = jnp.sin(ang)
    for k in range(4):
        def group_to(x, dst, k=k):
            shift = (32 * (dst - k)) % LANES
            return pltpu.roll(x, shift, 1) if shift else x
        cos_k = jnp.where(lane < 32, group_to(cos4, 0), group_to(cos4, 1))
        sin_k = jnp.where(lane < 96, -group_to(sin4, 2), group_to(sin4, 3))
        tab_ref[k * quarter:(k + 1) * quarter, :] = jnp.where(lane < 64, cos_k, sin_k)

    n = [_rmsnorm(x_ref[r, :], g_mix_ref[...]).astype(BF16) for r in rows]
    proj = [jnp.dot(ns, w_a_ref[...], preferred_element_type=F32) for ns in n]
    u = [jnp.dot(ns, w_u_ref[...], preferred_element_type=F32) for ns in n]
    g_q = g_q_ref[...] * q_scale
    cq = [_rmsnorm(pr[:, :Q_LORA], g_q).astype(BF16) for pr in proj]
    ckv = [_rmsnorm(pr[:, Q_LORA:kr_off], g_kv_ref[...]).astype(BF16) for pr in proj]

    def rope(pair, tab):
        t = pair * tab
        return (t + pltpu.roll(t, 64, 1))[:, :ROPE]

    for r, pr, cqs, ckvs in zip(rows, proj, cq, ckv):
        tab = tab_ref[r, :]
        q = jnp.dot(cqs, w_q_ref[...], preferred_element_type=F32)
        kn = jnp.dot(ckvs, w_k_ref[...], preferred_element_type=F32)
        vt_ref[:, r] = lax.dot_general(w_vt_ref[...], ckvs, (((1,), (1,)), ((), ())),
                                       preferred_element_type=F32).astype(BF16)
        kr = pr[:, kr_off:]
        kr2 = jnp.concatenate([kr, kr], axis=1)
        k_pe = rope(jnp.where(lane < 64, kr2, pltpu.roll(kr2, 32, 1)), tab).astype(BF16)
        for h in range(HEADS):
            q_ref[h, r, :NOPE] = q[:, h * NOPE:(h + 1) * NOPE].astype(BF16)
            pe = rope(q[:, HEADS * NOPE + h * LANES:HEADS * NOPE + (h + 1) * LANES], tab)
            q_ref[h, r, NOPE:] = pe.astype(BF16)
            k_ref[h, r, :NOPE] = kn[:, h * NOPE:(h + 1) * NOPE].astype(BF16)
            k_ref[h, r, NOPE:] = k_pe

    prev = halo_ref[...]
    halo_ref[...] = u[-1][sub - HALO:, :]
    for si, (r, us) in enumerate(zip(rows, u)):
        t_seq = (tile_in_seq * tm + si * sub) + lax.broadcasted_iota(jnp.int32, (sub, 1), 0)
        for g, w in enumerate(WINDOWS):
            cols = slice(g * POOL_CH, (g + 1) * POOL_CH)
            ug = us[:, cols]
            s = jnp.concatenate([prev[:, cols], ug], axis=0)
            for step in range(g + 1):
                s = s + pltpu.roll(s, 1 << step, 0)
            inv_cnt = 1.0 / jnp.minimum(t_seq + 1, w).astype(F32)
            p_ref[r, cols] = (s[HALO:, :] * inv_cnt - ug).astype(BF16)
        prev = us[sub - HALO:, :]


def _const_spec(shape):
    nd = len(shape)
    return pl.BlockSpec(shape, lambda *_: (0,) * nd, pipeline_mode=pl.Buffered(1))


def _front(x2, pos2, invf, g_mix, w_a, w_u, g_q, w_q, g_kv, w_k, w_vt, *, seq, tm, sub):
    t = x2.shape[0]
    tiles_per_seq = seq // tm
    q_scale = (QK ** -0.5) * math.log2(math.e)
    kern = functools.partial(_front_kernel, tm=tm, sub=sub, tiles_per_seq=tiles_per_seq,
                             q_scale=q_scale)
    return pl.pallas_call(
        kern,
        grid=(t // tm,),
        in_specs=[
            pl.BlockSpec((tm, D_MODEL), lambda i: (i, 0)),
            pl.BlockSpec((tm // 4, 4), lambda i: (i, 0)),
            _const_spec(invf.shape), _const_spec(g_mix.shape), _const_spec(w_a.shape),
            _const_spec(w_u.shape), _const_spec(g_q.shape), _const_spec(w_q.shape),
            _const_spec(g_kv.shape), _const_spec(w_k.shape), _const_spec(w_vt.shape),
        ],
        out_specs=[
            pl.BlockSpec((HEADS, tm, QK), lambda i: (0, i, 0)),
            pl.BlockSpec((HEADS, tm, QK), lambda i: (0, i, 0)),
            pl.BlockSpec((ATTN_W, tm), lambda i: (0, i)),
            pl.BlockSpec((tm, POOL_W), lambda i: (i, 0)),
        ],
        out_shape=[
            jax.ShapeDtypeStruct((HEADS, t, QK), BF16),
            jax.ShapeDtypeStruct((HEADS, t, QK), BF16),
            jax.ShapeDtypeStruct((ATTN_W, t), BF16),
            jax.ShapeDtypeStruct((t, POOL_W), BF16),
        ],
        scratch_shapes=[pltpu.VMEM((HALO, POOL_W), F32), pltpu.VMEM((tm, LANES), F32)],
        compiler_params=pltpu.CompilerParams(
            dimension_semantics=("arbitrary",), vmem_limit_bytes=FRONT_VMEM),
        name="front",
    )(x2, pos2, invf, g_mix, w_a, w_u, g_q, w_q, g_kv, w_k, w_vt)


def _attn_kernel(q_ref, k_ref, vt_ref, o_ref, acc_ref, *, tq, tk, nq):
    i = pl.program_id(1)
    half = tk // 2
    tri = (lax.broadcasted_iota(jnp.int32, (half, half), 1)
           >= lax.broadcasted_iota(jnp.int32, (half, half), 0))

    def scores(item):
        h, k0, rows, q0, cols, _ = item
        return lax.dot_general(k_ref[h, pl.ds(k0, rows), :], q_ref[h, q0:q0 + cols, :],
                               (((1,), (1,)), ((), ())), preferred_element_type=F32)

    def run(items, state, first=False):
        state = list(state)
        pending = {n: scores(items[n]) for n in range(min(LOOKAHEAD, len(items)))}
        for n, item in enumerate(items):
            if n + LOOKAHEAD < len(items):
                pending[n + LOOKAHEAD] = scores(items[n + LOOKAHEAD])
            h, k0, rows, q0, cols, masked = item
            s = pending.pop(n)
            if masked:
                lower = jnp.where(tri, s[:, :half], -jnp.inf)
                s = lower if cols == half else jnp.concatenate([lower, s[:, half:]], axis=1)
            vt = vt_ref[h * VDIM:(h + 1) * VDIM, pl.ds(k0, rows)]
            if first and state[h] is None:
                assert cols == tq
                m = jnp.max(s, axis=0, keepdims=True)
                p = jnp.exp2(s - m)
                state[h] = (m, jnp.sum(p, axis=0, keepdims=True))
                acc_ref[h] = jnp.dot(vt, p.astype(BF16), preferred_element_type=F32)
                continue
            m_all, l_all = state[h]
            m_old, l_old = m_all[:, q0:q0 + cols], l_all[:, q0:q0 + cols]
            m = jnp.maximum(m_old, jnp.max(s, axis=0, keepdims=True))
            alpha = jnp.exp2(m_old - m)
            p = jnp.exp2(s - m)
            l = alpha * l_old + jnp.sum(p, axis=0, keepdims=True)
            acc_ref[h, :, q0:q0 + cols] = (
                alpha * acc_ref[h, :, q0:q0 + cols]
                + jnp.dot(vt, p.astype(BF16), preferred_element_type=F32))
            if cols != tq:
                assert q0 + cols == tq
                m = jnp.concatenate([m_all[:, :q0], m], axis=1)
                l = jnp.concatenate([l_all[:, :q0], l], axis=1)
            state[h] = (m, l)
        return tuple(state)

    for qb in range(nq):
        @pl.when(i == qb)
        def _(qb=qb):
            items = ([(h, qb * tk, half, 0, tq, True) for h in range(HEADS)]
                     + [(h, qb * tk + half, half, half, tq - half, True) for h in range(HEADS)]
                     + [(h, j * tk, tk, 0, tq, False) for j in range(qb) for h in range(HEADS)])
            state = run(items, (None,) * HEADS, first=True)
            for h in range(HEADS):
                o_ref[:, h * VDIM:(h + 1) * VDIM] = (acc_ref[h] / state[h][1]).T.astype(BF16)


def _attn(q, k, vt, *, batch, seq, tq):
    t = vt.shape[1]
    nq = seq // tq
    kern = functools.partial(_attn_kernel, tq=tq, tk=tq, nq=nq)
    return pl.pallas_call(
        kern,
        grid=(batch, nq),
        in_specs=[
            pl.BlockSpec((HEADS, tq, QK), lambda b, i: (0, b * nq + i, 0)),
            pl.BlockSpec((HEADS, seq, QK), lambda b, i: (0, b, 0)),
            pl.BlockSpec((ATTN_W, seq), lambda b, i: (0, b)),
        ],
        out_specs=pl.BlockSpec((tq, ATTN_W), lambda b, i: (b * nq + i, 0)),
        out_shape=jax.ShapeDtypeStruct((t, ATTN_W), BF16),
        scratch_shapes=[pltpu.VMEM((HEADS, VDIM, tq), F32)],
        compiler_params=pltpu.CompilerParams(
            dimension_semantics=("arbitrary", "arbitrary"), vmem_limit_bytes=ATTN_VMEM),
        name="attn",
    )(q, k, vt)


def _back_kernel(x_ref, a_ref, p_ref, w_oa_ref, pool_w_ref, pool_scale_ref, w_op_ref, g_mlp_ref,
                 w_up_ref, w_down_ref, g_fin_ref, o_ref, w_pool_ref, *, tm, sub, fc):
    @pl.when(pl.program_id(0) == 0)
    def _():
        for g in range(GROUPS):
            ch = slice(g * POOL_CH, (g + 1) * POOL_CH)
            w_pool_ref[ch, :] = jnp.dot(
                pool_w_ref[g] * pool_scale_ref[:, ch], w_op_ref[ch, :],
                precision=lax.Precision.HIGHEST, preferred_element_type=F32).astype(BF16)

    rows = [pl.ds(r * sub, sub) for r in range(tm // sub)]
    h = [x_ref[r, :]
         + jnp.dot(a_ref[r, :], w_oa_ref[...], preferred_element_type=F32)
         + jnp.dot(p_ref[r, :], w_pool_ref[...], preferred_element_type=F32) for r in rows]
    n = [_rmsnorm(hs, g_mlp_ref[...]).astype(BF16) for hs in h]
    acc = [None] * len(rows)
    for c in range(D_FF // fc):
        for si, ns in enumerate(n):
            m = jnp.dot(ns, w_up_ref[:, c * fc:(c + 1) * fc], preferred_element_type=F32)
            r = jnp.square(jnp.maximum(m, 0.0)).astype(BF16)
            d = jnp.dot(r, w_down_ref[c * fc:(c + 1) * fc, :], preferred_element_type=F32)
            acc[si] = d if acc[si] is None else acc[si] + d
    for r, hs, accs in zip(rows, h, acc):
        o_ref[r, :] = _rmsnorm(hs + accs, g_fin_ref[...])


def _back(x2, a, p, w_oa, pool_w, pool_scale, w_op, g_mlp, w_up, w_down, g_fin, *, tm, sub, fc):
    t = x2.shape[0]
    kern = functools.partial(_back_kernel, tm=tm, sub=sub, fc=fc)
    return pl.pallas_call(
        kern,
        grid=(t // tm,),
        in_specs=[
            pl.BlockSpec((tm, D_MODEL), lambda i: (i, 0)),
            pl.BlockSpec((tm, ATTN_W), lambda i: (i, 0)),
            pl.BlockSpec((tm, POOL_W), lambda i: (i, 0)),
            _const_spec(w_oa.shape), _const_spec(pool_w.shape), _const_spec(pool_scale.shape),
            _const_spec(w_op.shape), _const_spec(g_mlp.shape),
            _const_spec(w_up.shape), _const_spec(w_down.shape), _const_spec(g_fin.shape),
        ],
        out_specs=pl.BlockSpec((tm, D_MODEL), lambda i: (i, 0)),
        out_shape=jax.ShapeDtypeStruct((t, D_MODEL), F32),
        scratch_shapes=[pltpu.VMEM((POOL_W, D_MODEL), BF16)],
        compiler_params=pltpu.CompilerParams(
            dimension_semantics=("arbitrary",), vmem_limit_bytes=BACK_VMEM),
        name="back",
    )(x2, a, p, w_oa, pool_w, pool_scale, w_op, g_mlp, w_up, w_down, g_fin)


def _half_swap(w):
    half = w.shape[-1] // 2
    return jnp.concatenate([w[..., half:], w[..., :half]], axis=-1)


def kernel(x, positions, norm_mix_g, w_in, q_norm_g, w_q_up, kv_norm_g, w_kv_up, pool_w,
           pool_scale, w_out, norm_mlp_g, w_mlp_up, w_mlp_down, norm_final_g):
    batch, seq, _ = x.shape
    t = batch * seq
    x2 = x.reshape(t, D_MODEL)
    pos2 = positions.reshape(t // FRONT_TM, 4, FRONT_TM // 4).transpose(0, 2, 1).reshape(t // 4, 4)
    inv_freq = 1.0 / (ROPE_THETA ** (jnp.arange(0, ROPE, 2, dtype=F32) / ROPE))
    invf = jnp.tile(inv_freq, 4).reshape(1, LANES)

    assert norm_mix_g.shape[0] == 1, "single-layer block only"
    u_col = Q_LORA + KV_LORA + ROPE
    w_a_l = w_in[0, :, :u_col].astype(BF16)
    w_u_l = w_in[0, :, u_col:].astype(BF16)
    wq = w_q_up[0].reshape(Q_LORA, HEADS, QK)
    wq_pe = wq[:, :, NOPE:]
    w_q_l = jnp.concatenate(
        [wq[:, :, :NOPE].reshape(Q_LORA, HEADS * NOPE),
         jnp.concatenate([wq_pe, _half_swap(wq_pe)], axis=-1).reshape(Q_LORA, HEADS * LANES)],
        axis=1).astype(BF16)
    wkv = w_kv_up[0].reshape(KV_LORA, HEADS, NOPE + VDIM)
    w_k_l = wkv[:, :, :NOPE].reshape(KV_LORA, HEADS * NOPE).astype(BF16)
    w_vt_l = wkv[:, :, NOPE:].reshape(KV_LORA, HEADS * VDIM).T.astype(BF16)

    q, k, vt, p = _front(
        x2, pos2, invf, norm_mix_g[0].reshape(1, D_MODEL), w_a_l, w_u_l,
        q_norm_g[0].reshape(1, Q_LORA), w_q_l, kv_norm_g[0].reshape(1, KV_LORA), w_k_l, w_vt_l,
        seq=seq, tm=FRONT_TM, sub=FRONT_SUB)
    a = _attn(q, k, vt, batch=batch, seq=seq, tq=ATTN_TQ)
    out = _back(
        x2, a, p, w_out[0, :ATTN_W].astype(BF16), pool_w[0], pool_scale[0].reshape(1, POOL_W),
        w_out[0, ATTN_W:], norm_mlp_g[0].reshape(1, D_MODEL), w_mlp_up[0].astype(BF16),
        w_mlp_down[0].astype(BF16), norm_final_g.reshape(1, D_MODEL), tm=BACK_TM,
        sub=BACK_SUB, fc=BACK_FC)
    return out.reshape(batch, seq, D_MODEL)
```

```python
import functools
import math

import jax
import jax.numpy as jnp
from jax import lax
from jax.experimental import pallas as pl
from jax.experimental.pallas import tpu as pltpu

D_MODEL = 1024
HEADS = 4
NOPE = 128
ROPE = 64
VDIM = 128
QK = NOPE + ROPE
ATTN_W = HEADS * VDIM
Q_LORA = 384
KV_LORA = 256
GROUPS = 4
WINDOWS = (2, 4, 8, 16)
POOL_W = 512
POOL_CH = 128
D_FF = 4096
ROPE_THETA = 10000.0
EPS = 1e-6
HALO = 16
LANES = 128
LOOKAHEAD = 3
EXP_HEADROOM = 64.0
MIB = 1024 * 1024
FRONT_VMEM = 34 * MIB
ATTN_VMEM = 24 * MIB
BACK_VMEM = 56 * MIB
FRONT_TM = 1024
FRONT_SUB = 256
ATTN_TQ = 512
BACK_TM = 1024
BACK_SUB = 256
BACK_FC = 1024

F32 = jnp.float32
BF16 = jnp.bfloat16


def _rmsnorm(x, g):
    ms = jnp.mean(x * x, axis=-1, keepdims=True)
    return x * lax.rsqrt(ms + EPS) * g


def _front_kernel(x_ref, pos_ref, invf_ref, g_mix_ref, w_a_ref, w_u_ref, g_q_ref, w_q_ref, g_kv_ref,
                  w_k_ref, w_vt_ref,
                  q_ref, k_ref, vt_ref, p_ref, halo_ref, tab_ref, *, tm, sub, tiles_per_seq,
                  q_scale):
    i = pl.program_id(0)
    tile_in_seq = i % tiles_per_seq
    rows = [pl.ds(r * sub, sub) for r in range(tm // sub)]
    lane = lax.broadcasted_iota(jnp.int32, (1, LANES), 1)
    kr_off = Q_LORA + KV_LORA

    @pl.when(tile_in_seq == 0)
    def _():
        halo_ref[...] = jnp.zeros_like(halo_ref)

    quarter = tm // 4
    pos = pos_ref[...]
    pos4 = jnp.where(lane < 64, jnp.where(lane < 32, pos[:, 0:1], pos[:, 1:2]),
                     jnp.where(lane < 96, pos[:, 2:3], pos[:, 3:4]))
    ang = pos4.astype(F32) * invf_ref[...]
    cos4 = jnp.cos(ang)
    sin4 = jnp.sin(ang)
    for k in range(4):
        def group_to(x, dst, k=k):
            shift = (32 * (dst - k)) % LANES
            return pltpu.roll(x, shift, 1) if shift else x
        cos_k = jnp.where(lane < 32, group_to(cos4, 0), group_to(cos4, 1))
        sin_k = jnp.where(lane < 96, -group_to(sin4, 2), group_to(sin4, 3))
        tab_ref[k * quarter:(k + 1) * quarter, :] = jnp.where(lane < 64, cos_k, sin_k)

    n = [_rmsnorm(x_ref[r, :], g_mix_ref[...]).astype(BF16) for r in rows]
    proj = [jnp.dot(ns, w_a_ref[...], preferred_element_type=F32) for ns in n]
    u = [jnp.dot(ns, w_u_ref[...], preferred_element_type=F32) for ns in n]
    g_q = g_q_ref[...] * q_scale
    cq = [_rmsnorm(pr[:, :Q_LORA], g_q).astype(BF16) for pr in proj]
    ckv = [_rmsnorm(pr[:, Q_LORA:kr_off], g_kv_ref[...]).astype(BF16) for pr in proj]

    def rope(pair, tab):
        t = pair * tab
        return (t + pltpu.roll(t, 64, 1))[:, :ROPE]

    for r, pr, cqs, ckvs in zip(rows, proj, cq, ckv):
        tab = tab_ref[r, :]
        q = jnp.dot(cqs, w_q_ref[...], preferred_element_type=F32)
        kn = jnp.dot(ckvs, w_k_ref[...], preferred_element_type=F32)
        vt_ref[:, r] = lax.dot_general(w_vt_ref[...], ckvs, (((1,), (1,)), ((), ())),
                                       preferred_element_type=F32).astype(BF16)
        kr = pr[:, kr_off:]
        kr2 = jnp.concatenate([kr, kr], axis=1)
        k_pe = rope(jnp.where(lane < 64, kr2, pltpu.roll(kr2, 32, 1)), tab).astype(BF16)
        for h in range(HEADS):
            q_ref[h, r, :NOPE] = q[:, h * NOPE:(h + 1) * NOPE].astype(BF16)
            pe = rope(q[:, HEADS * NOPE + h * LANES:HEADS * NOPE + (h + 1) * LANES], tab)
            q_ref[h, r, NOPE:] = pe.astype(BF16)
            k_ref[h, r, :NOPE] = kn[:, h * NOPE:(h + 1) * NOPE].astype(BF16)
            k_ref[h, r, NOPE:] = k_pe

    prev = halo_ref[...]
    halo_ref[...] = u[-1][sub - HALO:, :]
    for si, (r, us) in enumerate(zip(rows, u)):
        t_seq = (tile_in_seq * tm + si * sub) + lax.broadcasted_iota(jnp.int32, (sub, 1), 0)
        for g, w in enumerate(WINDOWS):
            cols = slice(g * POOL_CH, (g + 1) * POOL_CH)
            ug = us[:, cols]
            s = jnp.concatenate([prev[:, cols], ug], axis=0)
            for step in range(g + 1):
                s = s + pltpu.roll(s, 1 << step, 0)
            inv_cnt = 1.0 / jnp.minimum(t_seq + 1, w).astype(F32)
            p_ref[r, cols] = (s[HALO:, :] * inv_cnt - ug).astype(BF16)
        prev = us[sub - HALO:, :]


def _const_spec(shape):
    nd = len(shape)
    return pl.BlockSpec(shape, lambda *_: (0,) * nd, pipeline_mode=pl.Buffered(1))


def _front(x2, pos2, invf, g_mix, w_a, w_u, g_q, w_q, g_kv, w_k, w_vt, *, seq, tm, sub):
    t = x2.shape[0]
    tiles_per_seq = seq // tm
    q_scale = (QK ** -0.5) * math.log2(math.e)
    kern = functools.partial(_front_kernel, tm=tm, sub=sub, tiles_per_seq=tiles_per_seq,
                             q_scale=q_scale)
    return pl.pallas_call(
        kern,
        grid=(t // tm,),
        in_specs=[
            pl.BlockSpec((tm, D_MODEL), lambda i: (i, 0)),
            pl.BlockSpec((tm // 4, 4), lambda i: (i, 0)),
            _const_spec(invf.shape), _const_spec(g_mix.shape), _const_spec(w_a.shape),
            _const_spec(w_u.shape), _const_spec(g_q.shape), _const_spec(w_q.shape),
            _const_spec(g_kv.shape), _const_spec(w_k.shape), _const_spec(w_vt.shape),
        ],
        out_specs=[
            pl.BlockSpec((HEADS, tm, QK), lambda i: (0, i, 0)),
            pl.BlockSpec((HEADS, tm, QK), lambda i: (0, i, 0)),
            pl.BlockSpec((ATTN_W, tm), lambda i: (0, i)),
            pl.BlockSpec((tm, POOL_W), lambda i: (i, 0)),
        ],
        out_shape=[
            jax.ShapeDtypeStruct((HEADS, t, QK), BF16),
            jax.ShapeDtypeStruct((HEADS, t, QK), BF16),
            jax.ShapeDtypeStruct((ATTN_W, t), BF16),
            jax.ShapeDtypeStruct((t, POOL_W), BF16),
        ],
        scratch_shapes=[pltpu.VMEM((HALO, POOL_W), F32), pltpu.VMEM((tm, LANES), F32)],
        compiler_params=pltpu.CompilerParams(
            dimension_semantics=("arbitrary",), vmem_limit_bytes=FRONT_VMEM),
        name="front",
    )(x2, pos2, invf, g_mix, w_a, w_u, g_q, w_q, g_kv, w_k, w_vt)


def _attn_kernel(q_ref, k_ref, vt_ref, *rest, tq, tk, nq, fast):
    o_ref, excess_ref, acc_ref = rest[-3:]
    i = pl.program_id(1)
    half = tk // 2
    tri = (lax.broadcasted_iota(jnp.int32, (half, half), 1)
           >= lax.broadcasted_iota(jnp.int32, (half, half), 0))

    def scores(item):
        h, k0, rows, q0, cols, _ = item
        return lax.dot_general(k_ref[h, pl.ds(k0, rows), :], q_ref[h, q0:q0 + cols, :],
                               (((1,), (1,)), ((), ())), preferred_element_type=F32)

    def merge(full, part, q0):
        return part if q0 == 0 else jnp.concatenate([full[:, :q0], part], axis=1)

    def run(items, fast):
        state = [None] * HEADS
        pending = {n: scores(items[n]) for n in range(min(LOOKAHEAD, len(items)))}
        for n, item in enumerate(items):
            if n + LOOKAHEAD < len(items):
                pending[n + LOOKAHEAD] = scores(items[n + LOOKAHEAD])
            h, k0, rows, q0, cols, masked = item
            assert q0 + cols == tq
            s = pending.pop(n)
            if masked:
                lower = jnp.where(tri, s[:, :half], -jnp.inf)
                s = lower if cols == half else jnp.concatenate([lower, s[:, half:]], axis=1)
            vt = vt_ref[h * VDIM:(h + 1) * VDIM, pl.ds(k0, rows)]
            smax = jnp.max(s, axis=0, keepdims=True)
            if state[h] is None:
                assert cols == tq
                p = jnp.exp2(s - smax)
                state[h] = (smax, jnp.sum(p, axis=0, keepdims=True), smax)
                acc_ref[h] = jnp.dot(vt, p.astype(BF16), preferred_element_type=F32)
                continue
            ref_all, l_all, top_all = state[h]
            ref, l, top = (a[:, q0:] for a in state[h])
            if fast:
                p = jnp.exp2(s - ref)
                l = l + jnp.sum(p, axis=0, keepdims=True)
                top = jnp.maximum(top, smax)
                acc_ref[h, :, q0:] += jnp.dot(vt, p.astype(BF16), preferred_element_type=F32)
            else:
                new_ref = jnp.maximum(ref, smax)
                alpha = jnp.exp2(ref - new_ref)
                p = jnp.exp2(s - new_ref)
                l = alpha * l + jnp.sum(p, axis=0, keepdims=True)
                acc_ref[h, :, q0:] = (alpha * acc_ref[h, :, q0:]
                                      + jnp.dot(vt, p.astype(BF16), preferred_element_type=F32))
                ref = top = new_ref
            state[h] = (merge(ref_all, ref, q0), merge(l_all, l, q0), merge(top_all, top, q0))
        return state

    def finish(state):
        for h in range(HEADS):
            o_ref[:, h * VDIM:(h + 1) * VDIM] = (acc_ref[h] / state[h][1]).T.astype(BF16)

    for qb in range(nq):
        @pl.when(i == qb)
        def _(qb=qb):
            items = ([(h, qb * tk, half, 0, tq, True) for h in range(HEADS)]
                     + [(h, qb * tk + half, half, half, tq - half, True) for h in range(HEADS)]
                     + [(h, j * tk, tk, 0, tq, False) for j in range(qb) for h in range(HEADS)])
            state = run(items, fast)
            finish(state)
            excess = functools.reduce(jnp.maximum, [top - ref for ref, _, top in state])
            excess_ref[0] = functools.reduce(
                jnp.maximum, [excess[:, c:c + LANES] for c in range(0, tq, LANES)])


def _attn(q, k, vt, *, batch, seq, tq, fast, overwrite=None):
    t = vt.shape[1]
    nq = seq // tq
    kern = functools.partial(_attn_kernel, tq=tq, tk=tq, nq=nq, fast=fast)
    reuse = () if overwrite is None else (overwrite,)
    return pl.pallas_call(
        kern,
        grid=(batch, nq),
        in_specs=[
            pl.BlockSpec((HEADS, tq, QK), lambda b, i: (0, b * nq + i, 0)),
            pl.BlockSpec((HEADS, seq, QK), lambda b, i: (0, b, 0)),
            pl.BlockSpec((ATTN_W, seq), lambda b, i: (0, b)),
        ] + [pl.BlockSpec(memory_space=pl.ANY) for _ in reuse],
        input_output_aliases={3: 0} if reuse else {},
        out_specs=[pl.BlockSpec((tq, ATTN_W), lambda b, i: (b * nq + i, 0)),
                   pl.BlockSpec((1, 1, LANES), lambda b, i: (b * nq + i, 0, 0))],
        out_shape=[jax.ShapeDtypeStruct((t, ATTN_W), BF16),
                   jax.ShapeDtypeStruct((batch * nq, 1, LANES), F32)],
        scratch_shapes=[pltpu.VMEM((HEADS, VDIM, tq), F32)],
        compiler_params=pltpu.CompilerParams(
            dimension_semantics=("arbitrary", "arbitrary"), vmem_limit_bytes=ATTN_VMEM),
        name="attn",
    )(q, k, vt, *reuse)


def _back_kernel(x_ref, a_ref, p_ref, w_oa_ref, pool_w_ref, pool_scale_ref, w_op_ref, g_mlp_ref,
                 w_up_ref, w_down_ref, g_fin_ref, o_ref, w_pool_ref, *, tm, sub, fc):
    @pl.when(pl.program_id(0) == 0)
    def _():
        for g in range(GROUPS):
            ch = slice(g * POOL_CH, (g + 1) * POOL_CH)
            w_pool_ref[ch, :] = jnp.dot(
                pool_w_ref[g] * pool_scale_ref[:, ch], w_op_ref[ch, :],
                precision=lax.Precision.HIGHEST, preferred_element_type=F32).astype(BF16)

    rows = [pl.ds(r * sub, sub) for r in range(tm // sub)]
    h = [x_ref[r, :]
         + jnp.dot(a_ref[r, :], w_oa_ref[...], preferred_element_type=F32)
         + jnp.dot(p_ref[r, :], w_pool_ref[...], preferred_element_type=F32) for r in rows]
    n = [_rmsnorm(hs, g_mlp_ref[...]).astype(BF16) for hs in h]
    acc = [None] * len(rows)
    for c in range(D_FF // fc):
        for si, ns in enumerate(n):
            m = jnp.dot(ns, w_up_ref[:, c * fc:(c + 1) * fc], preferred_element_type=F32)
            r = jnp.square(jnp.maximum(m, 0.0)).astype(BF16)
            d = jnp.dot(r, w_down_ref[c * fc:(c + 1) * fc, :], preferred_element_type=F32)
            acc[si] = d if acc[si] is None else acc[si] + d
    for r, hs, accs in zip(rows, h, acc):
        o_ref[r, :] = _rmsnorm(hs + accs, g_fin_ref[...])


def _back(x2, a, p, w_oa, pool_w, pool_scale, w_op, g_mlp, w_up, w_down, g_fin, *, tm, sub, fc):
    t = x2.shape[0]
    kern = functools.partial(_back_kernel, tm=tm, sub=sub, fc=fc)
    return pl.pallas_call(
        kern,
        grid=(t // tm,),
        in_specs=[
            pl.BlockSpec((tm, D_MODEL), lambda i: (i, 0)),
            pl.BlockSpec((tm, ATTN_W), lambda i: (i, 0)),
            pl.BlockSpec((tm, POOL_W), lambda i: (i, 0)),
            _const_spec(w_oa.shape), _const_spec(pool_w.shape), _const_spec(pool_scale.shape),
            _const_spec(w_op.shape), _const_spec(g_mlp.shape),
            _const_spec(w_up.shape), _const_spec(w_down.shape), _const_spec(g_fin.shape),
        ],
        out_specs=pl.BlockSpec((tm, D_MODEL), lambda i: (i, 0)),
        out_shape=jax.ShapeDtypeStruct((t, D_MODEL), F32),
        scratch_shapes=[pltpu.VMEM((POOL_W, D_MODEL), BF16)],
        compiler_params=pltpu.CompilerParams(
            dimension_semantics=("arbitrary",), vmem_limit_bytes=BACK_VMEM),
        name="back",
    )(x2, a, p, w_oa, pool_w, pool_scale, w_op, g_mlp, w_up, w_down, g_fin)


def _half_swap(w):
    half = w.shape[-1] // 2
    return jnp.concatenate([w[..., half:], w[..., :half]], axis=-1)


def kernel(x, positions, norm_mix_g, w_in, q_norm_g, w_q_up, kv_norm_g, w_kv_up, pool_w,
           pool_scale, w_out, norm_mlp_g, w_mlp_up, w_mlp_down, norm_final_g):
    batch, seq, _ = x.shape
    t = batch * seq
    x2 = x.reshape(t, D_MODEL)
    pos2 = positions.reshape(t // FRONT_TM, 4, FRONT_TM // 4).transpose(0, 2, 1).reshape(t // 4, 4)
    inv_freq = 1.0 / (ROPE_THETA ** (jnp.arange(0, ROPE, 2, dtype=F32) / ROPE))
    invf = jnp.tile(inv_freq, 4).reshape(1, LANES)

    assert norm_mix_g.shape[0] == 1, "single-layer block only"
    u_col = Q_LORA + KV_LORA + ROPE
    w_a_l = w_in[0, :, :u_col].astype(BF16)
    w_u_l = w_in[0, :, u_col:].astype(BF16)
    wq = w_q_up[0].reshape(Q_LORA, HEADS, QK)
    wq_pe = wq[:, :, NOPE:]
    w_q_l = jnp.concatenate(
        [wq[:, :, :NOPE].reshape(Q_LORA, HEADS * NOPE),
         jnp.concatenate([wq_pe, _half_swap(wq_pe)], axis=-1).reshape(Q_LORA, HEADS * LANES)],
        axis=1).astype(BF16)
    wkv = w_kv_up[0].reshape(KV_LORA, HEADS, NOPE + VDIM)
    w_k_l = wkv[:, :, :NOPE].reshape(KV_LORA, HEADS * NOPE).astype(BF16)
    w_vt_l = wkv[:, :, NOPE:].reshape(KV_LORA, HEADS * VDIM).T.astype(BF16)

    q, k, vt, p = _front(
        x2, pos2, invf, norm_mix_g[0].reshape(1, D_MODEL), w_a_l, w_u_l,
        q_norm_g[0].reshape(1, Q_LORA), w_q_l, kv_norm_g[0].reshape(1, KV_LORA), w_k_l, w_vt_l,
        seq=seq, tm=FRONT_TM, sub=FRONT_SUB)
    a, excess = _attn(q, k, vt, batch=batch, seq=seq, tq=ATTN_TQ, fast=True)
    w_up_l, w_down_l, excess = lax.optimization_barrier(
        (w_mlp_up[0].astype(BF16), w_mlp_down[0].astype(BF16), excess))
    a = lax.cond(
        jnp.max(excess) > EXP_HEADROOM,
        lambda a: _attn(q, k, vt, batch=batch, seq=seq, tq=ATTN_TQ, fast=False, overwrite=a)[0],
        lambda a: a, a)
    out = _back(
        x2, a, p, w_out[0, :ATTN_W].astype(BF16), pool_w[0], pool_scale[0].reshape(1, POOL_W),
        w_out[0, ATTN_W:], norm_mlp_g[0].reshape(1, D_MODEL), w_up_l, w_down_l,
        norm_final_g.reshape(1, D_MODEL), tm=BACK_TM, sub=BACK_SUB, fc=BACK_FC)
    return out.reshape(batch, seq, D_MODEL)
```

```python
import functools
import math

import jax
import jax.numpy as jnp
from jax import lax
from jax.experimental import pallas as pl
from jax.experimental.pallas import tpu as pltpu

D_MODEL = 1024
HEADS = 4
NOPE = 128
ROPE = 64
VDIM = 128
QK = NOPE + ROPE
ATTN_W = HEADS * VDIM
Q_LORA = 384
KV_LORA = 256
GROUPS = 4
WINDOWS = (2, 4, 8, 16)
POOL_W = 512
POOL_CH = 128
D_FF = 4096
ROPE_THETA = 10000.0
EPS = 1e-6
HALO = 16
LANES = 128
LOOKAHEAD = 3
EXP_HEADROOM = 64.0
MIB = 1024 * 1024
FRONT_VMEM = 34 * MIB
ATTN_VMEM = 24 * MIB
BACK_VMEM = 56 * MIB
FRONT_TM = 1024
FRONT_SUB = 256
ATTN_TQ = 512
BACK_TM = 1024
BACK_SUB = 256
BACK_FC = 1024

F32 = jnp.float32
BF16 = jnp.bfloat16


def _rmsnorm(x, g):
    ms = jnp.mean(x * x, axis=-1, keepdims=True)
    return x * lax.rsqrt(ms + EPS) * g


def _front_kernel(x_ref, pos_ref, invf_ref, g_mix_ref, w_a_ref, w_u_ref, g_q_ref, w_q_ref, g_kv_ref,
                  w_k_ref, w_vt_ref,
                  q_ref, k_ref, vt_ref, p_ref, halo_ref, tab_ref, *, tm, sub, tiles_per_seq,
                  q_scale):
    i = pl.program_id(0)
    tile_in_seq = i % tiles_per_seq
    rows = [pl.ds(r * sub, sub) for r in range(tm // sub)]
    lane = lax.broadcasted_iota(jnp.int32, (1, LANES), 1)
    kr_off = Q_LORA + KV_LORA

    @pl.when(tile_in_seq == 0)
    def _():
        halo_ref[...] = jnp.zeros_like(halo_ref)

    quarter = tm // 4
    pos = pos_ref[...]
    pos4 = jnp.where(lane < 64, jnp.where(lane < 32, pos[:, 0:1], pos[:, 1:2]),
                     jnp.where(lane < 96, pos[:, 2:3], pos[:, 3:4]))
    ang = pos4.astype(F32) * invf_ref[...]
    cos4 = jnp.cos(ang)
    sin4 = jnp.sin(ang)
    for k in range(4):
        def group_to(x, dst, k=k):
            shift = (32 * (dst - k)) % LANES
            return pltpu.roll(x, shift, 1) if shift else x
        cos_k = jnp.where(lane < 32, group_to(cos4, 0), group_to(cos4, 1))
        sin_k = jnp.where(lane < 96, -group_to(sin4, 2), group_to(sin4, 3))
        tab_ref[k * quarter:(k + 1) * quarter, :] = jnp.where(lane < 64, cos_k, sin_k)

    n = [_rmsnorm(x_ref[r, :], g_mix_ref[...]).astype(BF16) for r in rows]
    proj = [jnp.dot(ns, w_a_ref[...], preferred_element_type=F32) for ns in n]
    u = [jnp.dot(ns, w_u_ref[...], preferred_element_type=F32) for ns in n]
    g_q = g_q_ref[...] * q_scale
    cq = [_rmsnorm(pr[:, :Q_LORA], g_q).astype(BF16) for pr in proj]
    ckv = [_rmsnorm(pr[:, Q_LORA:kr_off], g_kv_ref[...]).astype(BF16) for pr in proj]

    def rope(pair, tab):
        t = pair * tab
        return (t + pltpu.roll(t, 64, 1))[:, :ROPE]

    for r, pr, cqs, ckvs in zip(rows, proj, cq, ckv):
        tab = tab_ref[r, :]
        q = jnp.dot(cqs, w_q_ref[...], preferred_element_type=F32)
        kn = jnp.dot(ckvs, w_k_ref[...], preferred_element_type=F32)
        vt_ref[:, r] = lax.dot_general(w_vt_ref[...], ckvs, (((1,), (1,)), ((), ())),
                                       preferred_element_type=F32).astype(BF16)
        kr = pr[:, kr_off:]
        kr2 = jnp.concatenate([kr, kr], axis=1)
        k_pe = rope(jnp.where(lane < 64, kr2, pltpu.roll(kr2, 32, 1)), tab).astype(BF16)
        for h in range(HEADS):
            q_ref[h, r, :NOPE] = q[:, h * NOPE:(h + 1) * NOPE].astype(BF16)
            pe = rope(q[:, HEADS * NOPE + h * LANES:HEADS * NOPE + (h + 1) * LANES], tab)
            q_ref[h, r, NOPE:] = pe.astype(BF16)
            k_ref[h, r, :NOPE] = kn[:, h * NOPE:(h + 1) * NOPE].astype(BF16)
            k_ref[h, r, NOPE:] = k_pe

    prev = halo_ref[...]
    halo_ref[...] = u[-1][sub - HALO:, :]
    for si, (r, us) in enumerate(zip(rows, u)):
        t_seq = (tile_in_seq * tm + si * sub) + lax.broadcasted_iota(jnp.int32, (sub, 1), 0)
        for g, w in enumerate(WINDOWS):
            cols = slice(g * POOL_CH, (g + 1) * POOL_CH)
            ug = us[:, cols]
            s = jnp.concatenate([prev[:, cols], ug], axis=0)
            for step in range(g + 1):
                s = s + pltpu.roll(s, 1 << step, 0)
            inv_cnt = 1.0 / jnp.minimum(t_seq + 1, w).astype(F32)
            p_ref[r, cols] = (s[HALO:, :] * inv_cnt - ug).astype(BF16)
        prev = us[sub - HALO:, :]


def _const_spec(shape):
    nd = len(shape)
    return pl.BlockSpec(shape, lambda *_: (0,) * nd, pipeline_mode=pl.Buffered(1))


def _front(x2, pos2, invf, g_mix, w_a, w_u, g_q, w_q, g_kv, w_k, w_vt, *, seq, tm, sub):
    t = x2.shape[0]
    tiles_per_seq = seq // tm
    q_scale = (QK ** -0.5) * math.log2(math.e)
    kern = functools.partial(_front_kernel, tm=tm, sub=sub, tiles_per_seq=tiles_per_seq,
                             q_scale=q_scale)
    return pl.pallas_call(
        kern,
        grid=(t // tm,),
        in_specs=[
            pl.BlockSpec((tm, D_MODEL), lambda i: (i, 0)),
            pl.BlockSpec((tm // 4, 4), lambda i: (i, 0)),
            _const_spec(invf.shape), _const_spec(g_mix.shape), _const_spec(w_a.shape),
            _const_spec(w_u.shape), _const_spec(g_q.shape), _const_spec(w_q.shape),
            _const_spec(g_kv.shape), _const_spec(w_k.shape), _const_spec(w_vt.shape),
        ],
        out_specs=[
            pl.BlockSpec((HEADS, tm, QK), lambda i: (0, i, 0)),
            pl.BlockSpec((HEADS, tm, QK), lambda i: (0, i, 0)),
            pl.BlockSpec((ATTN_W, tm), lambda i: (0, i)),
            pl.BlockSpec((tm, POOL_W), lambda i: (i, 0)),
        ],
        out_shape=[
            jax.ShapeDtypeStruct((HEADS, t, QK), BF16),
            jax.ShapeDtypeStruct((HEADS, t, QK), BF16),
            jax.ShapeDtypeStruct((ATTN_W, t), BF16),
            jax.ShapeDtypeStruct((t, POOL_W), BF16),
        ],
        scratch_shapes=[pltpu.VMEM((HALO, POOL_W), F32), pltpu.VMEM((tm, LANES), F32)],
        compiler_params=pltpu.CompilerParams(
            dimension_semantics=("arbitrary",), vmem_limit_bytes=FRONT_VMEM),
        name="front",
    )(x2, pos2, invf, g_mix, w_a, w_u, g_q, w_q, g_kv, w_k, w_vt)


def _attn_kernel(q_ref, k_ref, vt_ref, *rest, tq, tk, nq, fast, ncast):
    cast_in, (o_ref, excess_ref), cast_out = rest[:ncast], rest[ncast:ncast + 2], rest[ncast + 2:-1]
    acc_ref = rest[-1]
    for src, dst in zip(cast_in, cast_out):
        dst[...] = src[...].astype(BF16)
    i = pl.program_id(1)
    half = tk // 2
    tri = (lax.broadcasted_iota(jnp.int32, (half, half), 1)
           >= lax.broadcasted_iota(jnp.int32, (half, half), 0))

    def scores(item):
        h, k0, rows, q0, cols, _ = item
        return lax.dot_general(k_ref[h, pl.ds(k0, rows), :], q_ref[h, q0:q0 + cols, :],
                               (((1,), (1,)), ((), ())), preferred_element_type=F32)

    def merge(full, part, q0):
        return part if q0 == 0 else jnp.concatenate([full[:, :q0], part], axis=1)

    def run(items, fast):
        state = [None] * HEADS
        pending = {n: scores(items[n]) for n in range(min(LOOKAHEAD, len(items)))}
        for n, item in enumerate(items):
            if n + LOOKAHEAD < len(items):
                pending[n + LOOKAHEAD] = scores(items[n + LOOKAHEAD])
            h, k0, rows, q0, cols, masked = item
            assert q0 + cols == tq
            s = pending.pop(n)
            if masked:
                lower = jnp.where(tri, s[:, :half], -jnp.inf)
                s = lower if cols == half else jnp.concatenate([lower, s[:, half:]], axis=1)
            vt = vt_ref[h * VDIM:(h + 1) * VDIM, pl.ds(k0, rows)]
            smax = jnp.max(s, axis=0, keepdims=True)
            if state[h] is None:
                assert cols == tq
                p = jnp.exp2(s - smax)
                state[h] = (smax, jnp.sum(p, axis=0, keepdims=True), smax)
                acc_ref[h] = jnp.dot(vt, p.astype(BF16), preferred_element_type=F32)
                continue
            ref_all, l_all, top_all = state[h]
            ref, l, top = (a[:, q0:] for a in state[h])
            if fast:
                p = jnp.exp2(s - ref)
                l = l + jnp.sum(p, axis=0, keepdims=True)
                top = jnp.maximum(top, smax)
                acc_ref[h, :, q0:] += jnp.dot(vt, p.astype(BF16), preferred_element_type=F32)
            else:
                new_ref = jnp.maximum(ref, smax)
                alpha = jnp.exp2(ref - new_ref)
                p = jnp.exp2(s - new_ref)
                l = alpha * l + jnp.sum(p, axis=0, keepdims=True)
                acc_ref[h, :, q0:] = (alpha * acc_ref[h, :, q0:]
                                      + jnp.dot(vt, p.astype(BF16), preferred_element_type=F32))
                ref = top = new_ref
            state[h] = (merge(ref_all, ref, q0), merge(l_all, l, q0), merge(top_all, top, q0))
        return state

    def finish(state):
        for h in range(HEADS):
            o_ref[:, h * VDIM:(h + 1) * VDIM] = (acc_ref[h] / state[h][1]).T.astype(BF16)

    for qb in range(nq):
        @pl.when(i == qb)
        def _(qb=qb):
            items = ([(h, qb * tk, half, 0, tq, True) for h in range(HEADS)]
                     + [(h, qb * tk + half, half, half, tq - half, True) for h in range(HEADS)]
                     + [(h, j * tk, tk, 0, tq, False) for j in range(qb) for h in range(HEADS)])
            state = run(items, fast)
            finish(state)
            excess = functools.reduce(jnp.maximum, [top - ref for ref, _, top in state])
            excess_ref[0] = functools.reduce(
                jnp.maximum, [excess[:, c:c + LANES] for c in range(0, tq, LANES)])


def _attn(q, k, vt, *, batch, seq, tq, fast, cast=()):
    t = vt.shape[1]
    nq = seq // tq
    steps = batch * nq
    kern = functools.partial(_attn_kernel, tq=tq, tk=tq, nq=nq, fast=fast, ncast=len(cast))
    cast_specs = [pl.BlockSpec((w.shape[0] // steps, w.shape[1]), lambda b, i: (b * nq + i, 0))
                  for w in cast]
    return pl.pallas_call(
        kern,
        grid=(batch, nq),
        in_specs=[
            pl.BlockSpec((HEADS, tq, QK), lambda b, i: (0, b * nq + i, 0)),
            pl.BlockSpec((HEADS, seq, QK), lambda b, i: (0, b, 0)),
            pl.BlockSpec((ATTN_W, seq), lambda b, i: (0, b)),
        ] + cast_specs,
        out_specs=[pl.BlockSpec((tq, ATTN_W), lambda b, i: (b * nq + i, 0)),
                   pl.BlockSpec((1, 1, LANES), lambda b, i: (b * nq + i, 0, 0))] + cast_specs,
        out_shape=[jax.ShapeDtypeStruct((t, ATTN_W), BF16),
                   jax.ShapeDtypeStruct((steps, 1, LANES), F32)]
        + [jax.ShapeDtypeStruct(w.shape, BF16) for w in cast],
        scratch_shapes=[pltpu.VMEM((HEADS, VDIM, tq), F32)],
        compiler_params=pltpu.CompilerParams(
            dimension_semantics=("arbitrary", "arbitrary"), vmem_limit_bytes=ATTN_VMEM),
        name="attn",
    )(q, k, vt, *cast)


def _back_kernel(x_ref, a_ref, p_ref, w_oa_ref, pool_w_ref, pool_scale_ref, w_op_ref, g_mlp_ref,
                 w_up_ref, w_down_ref, g_fin_ref, o_ref, w_pool_ref, *, tm, sub, fc):
    @pl.when(pl.program_id(0) == 0)
    def _():
        for g in range(GROUPS):
            ch = slice(g * POOL_CH, (g + 1) * POOL_CH)
            w_pool_ref[ch, :] = jnp.dot(
                pool_w_ref[g] * pool_scale_ref[:, ch], w_op_ref[ch, :],
                precision=lax.Precision.HIGHEST, preferred_element_type=F32).astype(BF16)

    rows = [pl.ds(r * sub, sub) for r in range(tm // sub)]
    h = [x_ref[r, :]
         + jnp.dot(a_ref[r, :], w_oa_ref[...], preferred_element_type=F32)
         + jnp.dot(p_ref[r, :], w_pool_ref[...], preferred_element_type=F32) for r in rows]
    n = [_rmsnorm(hs, g_mlp_ref[...]).astype(BF16) for hs in h]
    acc = [None] * len(rows)
    for c in range(D_FF // fc):
        for si, ns in enumerate(n):
            m = jnp.dot(ns, w_up_ref[:, c * fc:(c + 1) * fc], preferred_element_type=F32)
            r = jnp.square(jnp.maximum(m, 0.0)).astype(BF16)
            d = jnp.dot(r, w_down_ref[c * fc:(c + 1) * fc, :], preferred_element_type=F32)
            acc[si] = d if acc[si] is None else acc[si] + d
    for r, hs, accs in zip(rows, h, acc):
        o_ref[r, :] = _rmsnorm(hs + accs, g_fin_ref[...])


def _back(x2, a, p, w_oa, pool_w, pool_scale, w_op, g_mlp, w_up, w_down, g_fin, *, tm, sub, fc):
    t = x2.shape[0]
    kern = functools.partial(_back_kernel, tm=tm, sub=sub, fc=fc)
    return pl.pallas_call(
        kern,
        grid=(t // tm,),
        in_specs=[
            pl.BlockSpec((tm, D_MODEL), lambda i: (i, 0)),
            pl.BlockSpec((tm, ATTN_W), lambda i: (i, 0)),
            pl.BlockSpec((tm, POOL_W), lambda i: (i, 0)),
            _const_spec(w_oa.shape), _const_spec(pool_w.shape), _const_spec(pool_scale.shape),
            _const_spec(w_op.shape), _const_spec(g_mlp.shape),
            _const_spec(w_up.shape), _const_spec(w_down.shape), _const_spec(g_fin.shape),
        ],
        out_specs=pl.BlockSpec((tm, D_MODEL), lambda i: (i, 0)),
        out_shape=jax.ShapeDtypeStruct((t, D_MODEL), F32),
        scratch_shapes=[pltpu.VMEM((POOL_W, D_MODEL), BF16)],
        compiler_params=pltpu.CompilerParams(
            dimension_semantics=("arbitrary",), vmem_limit_bytes=BACK_VMEM),
        name="back",
    )(x2, a, p, w_oa, pool_w, pool_scale, w_op, g_mlp, w_up, w_down, g_fin)


def _half_swap(w):
    half = w.shape[-1] // 2
    return jnp.concatenate([w[..., half:], w[..., :half]], axis=-1)


def kernel(x, positions, norm_mix_g, w_in, q_norm_g, w_q_up, kv_norm_g, w_kv_up, pool_w,
           pool_scale, w_out, norm_mlp_g, w_mlp_up, w_mlp_down, norm_final_g):
    batch, seq, _ = x.shape
    t = batch * seq
    x2 = x.reshape(t, D_MODEL)
    pos2 = positions.reshape(t // FRONT_TM, 4, FRONT_TM // 4).transpose(0, 2, 1).reshape(t // 4, 4)
    inv_freq = 1.0 / (ROPE_THETA ** (jnp.arange(0, ROPE, 2, dtype=F32) / ROPE))
    invf = jnp.tile(inv_freq, 4).reshape(1, LANES)

    assert norm_mix_g.shape[0] == 1, "single-layer block only"
    u_col = Q_LORA + KV_LORA + ROPE
    w_a_l = w_in[0, :, :u_col].astype(BF16)
    w_u_l = w_in[0, :, u_col:].astype(BF16)
    wq = w_q_up[0].reshape(Q_LORA, HEADS, QK)
    wq_pe = wq[:, :, NOPE:]
    w_q_l = jnp.concatenate(
        [wq[:, :, :NOPE].reshape(Q_LORA, HEADS * NOPE),
         jnp.concatenate([wq_pe, _half_swap(wq_pe)], axis=-1).reshape(Q_LORA, HEADS * LANES)],
        axis=1).astype(BF16)
    wkv = w_kv_up[0].reshape(KV_LORA, HEADS, NOPE + VDIM)
    w_k_l = wkv[:, :, :NOPE].reshape(KV_LORA, HEADS * NOPE).astype(BF16)
    w_vt_l = wkv[:, :, NOPE:].reshape(KV_LORA, HEADS * VDIM).T.astype(BF16)

    q, k, vt, p = _front(
        x2, pos2, invf, norm_mix_g[0].reshape(1, D_MODEL), w_a_l, w_u_l,
        q_norm_g[0].reshape(1, Q_LORA), w_q_l, kv_norm_g[0].reshape(1, KV_LORA), w_k_l, w_vt_l,
        seq=seq, tm=FRONT_TM, sub=FRONT_SUB)
    a, excess, w_up_l, w_down_l = _attn(q, k, vt, batch=batch, seq=seq, tq=ATTN_TQ, fast=True,
                                        cast=(w_mlp_up[0], w_mlp_down[0]))
    w_oa_l = w_out[0, :ATTN_W].astype(BF16)

    def finish(a):
        return _back(
            x2, a, p, w_oa_l, pool_w[0], pool_scale[0].reshape(1, POOL_W), w_out[0, ATTN_W:],
            norm_mlp_g[0].reshape(1, D_MODEL), w_up_l, w_down_l, norm_final_g.reshape(1, D_MODEL),
            tm=BACK_TM, sub=BACK_SUB, fc=BACK_FC)

    out = lax.cond(
        jnp.max(excess) > EXP_HEADROOM,
        lambda: finish(_attn(q, k, vt, batch=batch, seq=seq, tq=ATTN_TQ, fast=False)[0]),
        lambda: finish(a))
    return out.reshape(batch, seq, D_MODEL)
```

```python
import functools
import math

import jax
import jax.numpy as jnp
from jax import lax
from jax.experimental import pallas as pl
from jax.experimental.pallas import tpu as pltpu

D_MODEL = 1024
HEADS = 4
NOPE = 128
ROPE = 64
VDIM = 128
QK = NOPE + ROPE
ATTN_W = HEADS * VDIM
Q_LORA = 384
KV_LORA = 256
GROUPS = 4
WINDOWS = (2, 4, 8, 16)
POOL_W = 512
POOL_CH = 128
D_FF = 4096
ROPE_THETA = 10000.0
EPS = 1e-6
HALO = 16
LANES = 128
LOOKAHEAD = 3
EXP_HEADROOM = 64.0
MIB = 1024 * 1024
FRONT_VMEM = 60 * MIB
ATTN_VMEM = 24 * MIB
BACK_VMEM = 56 * MIB
FRONT_TM = 2048
FRONT_SUB = 256
ATTN_TQ = 512
BACK_TM = 1024
BACK_SUB = 256
BACK_FC = 1024

F32 = jnp.float32
BF16 = jnp.bfloat16


def _rmsnorm(x, g):
    ms = jnp.mean(x * x, axis=-1, keepdims=True)
    return x * lax.rsqrt(ms + EPS) * g


def _front_kernel(x_ref, pos_ref, invf_ref, g_mix_ref, w_a_ref, w_u_ref, g_q_ref, w_q_ref, g_kv_ref,
                  w_k_ref, w_vt_ref,
                  q_ref, k_ref, vt_ref, p_ref, halo_ref, tab_ref, *, tm, sub, tiles_per_seq,
                  q_scale):
    i = pl.program_id(0)
    tile_in_seq = i % tiles_per_seq
    rows = [pl.ds(r * sub, sub) for r in range(tm // sub)]
    lane = lax.broadcasted_iota(jnp.int32, (1, LANES), 1)
    kr_off = Q_LORA + KV_LORA

    @pl.when(tile_in_seq == 0)
    def _():
        halo_ref[...] = jnp.zeros_like(halo_ref)

    quarter = tm // 4
    pos = pos_ref[...]
    pos4 = jnp.where(lane < 64, jnp.where(lane < 32, pos[:, 0:1], pos[:, 1:2]),
                     jnp.where(lane < 96, pos[:, 2:3], pos[:, 3:4]))
    ang = pos4.astype(F32) * invf_ref[...]
    cos4 = jnp.cos(ang)
    sin4 = jnp.sin(ang)
    for k in range(4):
        def group_to(x, dst, k=k):
            shift = (32 * (dst - k)) % LANES
            return pltpu.roll(x, shift, 1) if shift else x
        cos_k = jnp.where(lane < 32, group_to(cos4, 0), group_to(cos4, 1))
        sin_k = jnp.where(lane < 96, -group_to(sin4, 2), group_to(sin4, 3))
        tab_ref[k * quarter:(k + 1) * quarter, :] = jnp.where(lane < 64, cos_k, sin_k)

    n = [_rmsnorm(x_ref[r, :], g_mix_ref[...]).astype(BF16) for r in rows]
    proj = [jnp.dot(ns, w_a_ref[...], preferred_element_type=F32) for ns in n]
    u = [jnp.dot(ns, w_u_ref[...], preferred_element_type=F32) for ns in n]
    g_q = g_q_ref[...] * q_scale
    cq = [_rmsnorm(pr[:, :Q_LORA], g_q).astype(BF16) for pr in proj]
    ckv = [_rmsnorm(pr[:, Q_LORA:kr_off], g_kv_ref[...]).astype(BF16) for pr in proj]

    def rope(pair, tab):
        t = pair * tab
        return (t + pltpu.roll(t, 64, 1))[:, :ROPE]

    for r, pr, cqs, ckvs in zip(rows, proj, cq, ckv):
        tab = tab_ref[r, :]
        q = jnp.dot(cqs, w_q_ref[...], preferred_element_type=F32)
        kn = jnp.dot(ckvs, w_k_ref[...], preferred_element_type=F32)
        vt_ref[:, r] = lax.dot_general(w_vt_ref[...], ckvs, (((1,), (1,)), ((), ())),
                                       preferred_element_type=F32).astype(BF16)
        kr = pr[:, kr_off:]
        kr2 = jnp.concatenate([kr, kr], axis=1)
        k_pe = rope(jnp.where(lane < 64, kr2, pltpu.roll(kr2, 32, 1)), tab).astype(BF16)
        for h in range(HEADS):
            q_ref[h, r, :NOPE] = q[:, h * NOPE:(h + 1) * NOPE].astype(BF16)
            pe = rope(q[:, HEADS * NOPE + h * LANES:HEADS * NOPE + (h + 1) * LANES], tab)
            q_ref[h, r, NOPE:] = pe.astype(BF16)
            k_ref[h, r, :NOPE] = kn[:, h * NOPE:(h + 1) * NOPE].astype(BF16)
            k_ref[h, r, NOPE:] = k_pe

    prev = halo_ref[...]
    halo_ref[...] = u[-1][sub - HALO:, :]
    for si, (r, us) in enumerate(zip(rows, u)):
        t_seq = (tile_in_seq * tm + si * sub) + lax.broadcasted_iota(jnp.int32, (sub, 1), 0)
        for g, w in enumerate(WINDOWS):
            cols = slice(g * POOL_CH, (g + 1) * POOL_CH)
            ug = us[:, cols]
            s = jnp.concatenate([prev[:, cols], ug], axis=0)
            for step in range(g + 1):
                s = s + pltpu.roll(s, 1 << step, 0)
            inv_cnt = 1.0 / jnp.minimum(t_seq + 1, w).astype(F32)
            p_ref[r, cols] = (s[HALO:, :] * inv_cnt - ug).astype(BF16)
        prev = us[sub - HALO:, :]


def _const_spec(shape):
    nd = len(shape)
    return pl.BlockSpec(shape, lambda *_: (0,) * nd, pipeline_mode=pl.Buffered(1))


def _front(x2, pos2, invf, g_mix, w_a, w_u, g_q, w_q, g_kv, w_k, w_vt, *, seq, tm, sub):
    t = x2.shape[0]
    tiles_per_seq = seq // tm
    q_scale = (QK ** -0.5) * math.log2(math.e)
    kern = functools.partial(_front_kernel, tm=tm, sub=sub, tiles_per_seq=tiles_per_seq,
                             q_scale=q_scale)
    return pl.pallas_call(
        kern,
        grid=(t // tm,),
        in_specs=[
            pl.BlockSpec((tm, D_MODEL), lambda i: (i, 0)),
            pl.BlockSpec((tm // 4, 4), lambda i: (i, 0)),
            _const_spec(invf.shape), _const_spec(g_mix.shape), _const_spec(w_a.shape),
            _const_spec(w_u.shape), _const_spec(g_q.shape), _const_spec(w_q.shape),
            _const_spec(g_kv.shape), _const_spec(w_k.shape), _const_spec(w_vt.shape),
        ],
        out_specs=[
            pl.BlockSpec((HEADS, tm, QK), lambda i: (0, i, 0)),
            pl.BlockSpec((HEADS, tm, QK), lambda i: (0, i, 0)),
            pl.BlockSpec((ATTN_W, tm), lambda i: (0, i)),
            pl.BlockSpec((tm, POOL_W), lambda i: (i, 0)),
        ],
        out_shape=[
            jax.ShapeDtypeStruct((HEADS, t, QK), BF16),
            jax.ShapeDtypeStruct((HEADS, t, QK), BF16),
            jax.ShapeDtypeStruct((ATTN_W, t), BF16),
            jax.ShapeDtypeStruct((t, POOL_W), BF16),
        ],
        scratch_shapes=[pltpu.VMEM((HALO, POOL_W), F32), pltpu.VMEM((tm, LANES), F32)],
        compiler_params=pltpu.CompilerParams(
            dimension_semantics=("arbitrary",), vmem_limit_bytes=FRONT_VMEM),
        name="front",
    )(x2, pos2, invf, g_mix, w_a, w_u, g_q, w_q, g_kv, w_k, w_vt)


def _attn_kernel(q_ref, k_ref, vt_ref, *rest, tq, tk, nq, fast):
    o_ref, excess_ref, acc_ref = rest[-3:]
    i = pl.program_id(1)
    half = tk // 2
    tri = (lax.broadcasted_iota(jnp.int32, (half, half), 1)
           >= lax.broadcasted_iota(jnp.int32, (half, half), 0))

    def scores(item):
        h, k0, rows, q0, cols, _ = item
        return lax.dot_general(k_ref[h, pl.ds(k0, rows), :], q_ref[h, q0:q0 + cols, :],
                               (((1,), (1,)), ((), ())), preferred_element_type=F32)

    def merge(full, part, q0):
        return part if q0 == 0 else jnp.concatenate([full[:, :q0], part], axis=1)

    def run(items, fast):
        state = [None] * HEADS
        pending = {n: scores(items[n]) for n in range(min(LOOKAHEAD, len(items)))}
        for n, item in enumerate(items):
            if n + LOOKAHEAD < len(items):
                pending[n + LOOKAHEAD] = scores(items[n + LOOKAHEAD])
            h, k0, rows, q0, cols, masked = item
            assert q0 + cols == tq
            s = pending.pop(n)
            if masked:
                lower = jnp.where(tri, s[:, :half], -jnp.inf)
                s = lower if cols == half else jnp.concatenate([lower, s[:, half:]], axis=1)
            vt = vt_ref[h * VDIM:(h + 1) * VDIM, pl.ds(k0, rows)]
            smax = jnp.max(s, axis=0, keepdims=True)
            if state[h] is None:
                assert cols == tq
                p = jnp.exp2(s - smax)
                state[h] = (smax, jnp.sum(p, axis=0, keepdims=True), smax)
                acc_ref[h] = jnp.dot(vt, p.astype(BF16), preferred_element_type=F32)
                continue
            ref_all, l_all, top_all = state[h]
            ref, l, top = (a[:, q0:] for a in state[h])
            if fast:
                p = jnp.exp2(s - ref)
                l = l + jnp.sum(p, axis=0, keepdims=True)
                top = jnp.maximum(top, smax)
                acc_ref[h, :, q0:] += jnp.dot(vt, p.astype(BF16), preferred_element_type=F32)
            else:
                new_ref = jnp.maximum(ref, smax)
                alpha = jnp.exp2(ref - new_ref)
                p = jnp.exp2(s - new_ref)
                l = alpha * l + jnp.sum(p, axis=0, keepdims=True)
                acc_ref[h, :, q0:] = (alpha * acc_ref[h, :, q0:]
                                      + jnp.dot(vt, p.astype(BF16), preferred_element_type=F32))
                ref = top = new_ref
            state[h] = (merge(ref_all, ref, q0), merge(l_all, l, q0), merge(top_all, top, q0))
        return state

    def finish(state):
        for h in range(HEADS):
            o_ref[:, h * VDIM:(h + 1) * VDIM] = (acc_ref[h] / state[h][1]).T.astype(BF16)

    for qb in range(nq):
        @pl.when(i == qb)
        def _(qb=qb):
            items = ([(h, qb * tk, half, 0, tq, True) for h in range(HEADS)]
                     + [(h, qb * tk + half, half, half, tq - half, True) for h in range(HEADS)]
                     + [(h, j * tk, tk, 0, tq, False) for j in range(qb) for h in range(HEADS)])
            state = run(items, fast)
            finish(state)
            excess = functools.reduce(jnp.maximum, [top - ref for ref, _, top in state])
            excess_ref[0] = functools.reduce(
                jnp.maximum, [excess[:, c:c + LANES] for c in range(0, tq, LANES)])


def _attn(q, k, vt, *, batch, seq, tq, fast, overwrite=None):
    t = vt.shape[1]
    nq = seq // tq
    kern = functools.partial(_attn_kernel, tq=tq, tk=tq, nq=nq, fast=fast)
    reuse = () if overwrite is None else (overwrite,)
    return pl.pallas_call(
        kern,
        grid=(batch, nq),
        in_specs=[
            pl.BlockSpec((HEADS, tq, QK), lambda b, i: (0, b * nq + i, 0)),
            pl.BlockSpec((HEADS, seq, QK), lambda b, i: (0, b, 0)),
            pl.BlockSpec((ATTN_W, seq), lambda b, i: (0, b)),
        ] + [pl.BlockSpec(memory_space=pl.ANY) for _ in reuse],
        input_output_aliases={3: 0} if reuse else {},
        out_specs=[pl.BlockSpec((tq, ATTN_W), lambda b, i: (b * nq + i, 0)),
                   pl.BlockSpec((1, 1, LANES), lambda b, i: (b * nq + i, 0, 0))],
        out_shape=[jax.ShapeDtypeStruct((t, ATTN_W), BF16),
                   jax.ShapeDtypeStruct((batch * nq, 1, LANES), F32)],
        scratch_shapes=[pltpu.VMEM((HEADS, VDIM, tq), F32)],
        compiler_params=pltpu.CompilerParams(
            dimension_semantics=("arbitrary", "arbitrary"), vmem_limit_bytes=ATTN_VMEM),
        name="attn",
    )(q, k, vt, *reuse)


def _back_kernel(x_ref, a_ref, p_ref, w_oa_ref, pool_w_ref, pool_scale_ref, w_op_ref, g_mlp_ref,
                 w_up_ref, w_down_ref, g_fin_ref, o_ref, w_pool_ref, *, tm, sub, fc):
    @pl.when(pl.program_id(0) == 0)
    def _():
        for g in range(GROUPS):
            ch = slice(g * POOL_CH, (g + 1) * POOL_CH)
            w_pool_ref[ch, :] = jnp.dot(
                pool_w_ref[g] * pool_scale_ref[:, ch], w_op_ref[ch, :],
                precision=lax.Precision.HIGHEST, preferred_element_type=F32).astype(BF16)

    rows = [pl.ds(r * sub, sub) for r in range(tm // sub)]
    h = [x_ref[r, :]
         + jnp.dot(a_ref[r, :], w_oa_ref[...], preferred_element_type=F32)
         + jnp.dot(p_ref[r, :], w_pool_ref[...], preferred_element_type=F32) for r in rows]
    n = [_rmsnorm(hs, g_mlp_ref[...]).astype(BF16) for hs in h]
    acc = [None] * len(rows)
    for c in range(D_FF // fc):
        for si, ns in enumerate(n):
            m = jnp.dot(ns, w_up_ref[:, c * fc:(c + 1) * fc], preferred_element_type=F32)
            r = jnp.square(jnp.maximum(m, 0.0)).astype(BF16)
            d = jnp.dot(r, w_down_ref[c * fc:(c + 1) * fc, :], preferred_element_type=F32)
            acc[si] = d if acc[si] is None else acc[si] + d
    for r, hs, accs in zip(rows, h, acc):
        o_ref[r, :] = _rmsnorm(hs + accs, g_fin_ref[...])


def _back(x2, a, p, w_oa, pool_w, pool_scale, w_op, g_mlp, w_up, w_down, g_fin, *, tm, sub, fc):
    t = x2.shape[0]
    kern = functools.partial(_back_kernel, tm=tm, sub=sub, fc=fc)
    return pl.pallas_call(
        kern,
        grid=(t // tm,),
        in_specs=[
            pl.BlockSpec((tm, D_MODEL), lambda i: (i, 0)),
            pl.BlockSpec((tm, ATTN_W), lambda i: (i, 0)),
            pl.BlockSpec((tm, POOL_W), lambda i: (i, 0)),
            _const_spec(w_oa.shape), _const_spec(pool_w.shape), _const_spec(pool_scale.shape),
            _const_spec(w_op.shape), _const_spec(g_mlp.shape),
            _const_spec(w_up.shape), _const_spec(w_down.shape), _const_spec(g_fin.shape),
        ],
        out_specs=pl.BlockSpec((tm, D_MODEL), lambda i: (i, 0)),
        out_shape=jax.ShapeDtypeStruct((t, D_MODEL), F32),
        scratch_shapes=[pltpu.VMEM((POOL_W, D_MODEL), BF16)],
        compiler_params=pltpu.CompilerParams(
            dimension_semantics=("arbitrary",), vmem_limit_bytes=BACK_VMEM),
        name="back",
    )(x2, a, p, w_oa, pool_w, pool_scale, w_op, g_mlp, w_up, w_down, g_fin)


def _half_swap(w):
    half = w.shape[-1] // 2
    return jnp.concatenate([w[..., half:], w[..., :half]], axis=-1)


def kernel(x, positions, norm_mix_g, w_in, q_norm_g, w_q_up, kv_norm_g, w_kv_up, pool_w,
           pool_scale, w_out, norm_mlp_g, w_mlp_up, w_mlp_down, norm_final_g):
    batch, seq, _ = x.shape
    t = batch * seq
    x2 = x.reshape(t, D_MODEL)
    pos2 = positions.reshape(t // FRONT_TM, 4, FRONT_TM // 4).transpose(0, 2, 1).reshape(t // 4, 4)
    inv_freq = 1.0 / (ROPE_THETA ** (jnp.arange(0, ROPE, 2, dtype=F32) / ROPE))
    invf = jnp.tile(inv_freq, 4).reshape(1, LANES)

    assert norm_mix_g.shape[0] == 1, "single-layer block only"
    u_col = Q_LORA + KV_LORA + ROPE
    w_a_l = w_in[0, :, :u_col].astype(BF16)
    w_u_l = w_in[0, :, u_col:].astype(BF16)
    wq = w_q_up[0].reshape(Q_LORA, HEADS, QK)
    wq_pe = wq[:, :, NOPE:]
    w_q_l = jnp.concatenate(
        [wq[:, :, :NOPE].reshape(Q_LORA, HEADS * NOPE),
         jnp.concatenate([wq_pe, _half_swap(wq_pe)], axis=-1).reshape(Q_LORA, HEADS * LANES)],
        axis=1).astype(BF16)
    wkv = w_kv_up[0].reshape(KV_LORA, HEADS, NOPE + VDIM)
    w_k_l = wkv[:, :, :NOPE].reshape(KV_LORA, HEADS * NOPE).astype(BF16)
    w_vt_l = wkv[:, :, NOPE:].reshape(KV_LORA, HEADS * VDIM).T.astype(BF16)

    q, k, vt, p = _front(
        x2, pos2, invf, norm_mix_g[0].reshape(1, D_MODEL), w_a_l, w_u_l,
        q_norm_g[0].reshape(1, Q_LORA), w_q_l, kv_norm_g[0].reshape(1, KV_LORA), w_k_l, w_vt_l,
        seq=seq, tm=FRONT_TM, sub=FRONT_SUB)
    a, excess = _attn(q, k, vt, batch=batch, seq=seq, tq=ATTN_TQ, fast=True)
    w_up_l, w_down_l, excess = lax.optimization_barrier(
        (w_mlp_up[0].astype(BF16), w_mlp_down[0].astype(BF16), excess))
    a = lax.cond(
        jnp.max(excess) > EXP_HEADROOM,
        lambda a: _attn(q, k, vt, batch=batch, seq=seq, tq=ATTN_TQ, fast=False, overwrite=a)[0],
        lambda a: a, a)
    out = _back(
        x2, a, p, w_out[0, :ATTN_W].astype(BF16), pool_w[0], pool_scale[0].reshape(1, POOL_W),
        w_out[0, ATTN_W:], norm_mlp_g[0].reshape(1, D_MODEL), w_up_l, w_down_l,
        norm_final_g.reshape(1, D_MODEL), tm=BACK_TM, sub=BACK_SUB, fc=BACK_FC)
    return out.reshape(batch, seq, D_MODEL)
```

```python
import functools
import math

import jax
import jax.numpy as jnp
from jax import lax
from jax.experimental import pallas as pl
from jax.experimental.pallas import tpu as pltpu

D_MODEL = 1024
HEADS = 4
NOPE = 128
ROPE = 64
VDIM = 128
QK = NOPE + ROPE
ATTN_W = HEADS * VDIM
Q_LORA = 384
KV_LORA = 256
GROUPS = 4
WINDOWS = (2, 4, 8, 16)
POOL_W = 512
POOL_CH = 128
D_FF = 4096
ROPE_THETA = 10000.0
EPS = 1e-6
HALO = 16
LANES = 128
LOOKAHEAD = 3
EXP_HEADROOM = 64.0
MIB = 1024 * 1024
FRONT_VMEM = 60 * MIB
ATTN_VMEM = 24 * MIB
BACK_VMEM = 56 * MIB
FRONT_TM = 2048
FRONT_SUB = 256
ATTN_TQ = 512
BACK_TM = 1024
BACK_SUB = 256
BACK_FC = 1024

F32 = jnp.float32
BF16 = jnp.bfloat16


def _rmsnorm(x, g):
    ms = jnp.mean(x * x, axis=-1, keepdims=True)
    return x * lax.rsqrt(ms + EPS) * g


def _front_kernel(x_ref, pos_ref, invf_ref, g_mix_ref, w_a_ref, w_u_ref, g_q_ref, w_q_ref, g_kv_ref,
                  w_k_ref, w_vt_ref,
                  q_ref, k_ref, vt_ref, p_ref, halo_ref, tab_ref, *, tm, sub, tiles_per_seq,
                  q_scale):
    i = pl.program_id(0)
    tile_in_seq = i % tiles_per_seq
    rows = [pl.ds(r * sub, sub) for r in range(tm // sub)]
    lane = lax.broadcasted_iota(jnp.int32, (1, LANES), 1)
    kr_off = Q_LORA + KV_LORA

    @pl.when(tile_in_seq == 0)
    def _():
        halo_ref[...] = jnp.zeros_like(halo_ref)

    quarter = tm // 4
    pos = pos_ref[...]
    pos4 = jnp.where(lane < 64, jnp.where(lane < 32, pos[:, 0:1], pos[:, 1:2]),
                     jnp.where(lane < 96, pos[:, 2:3], pos[:, 3:4]))
    ang = pos4.astype(F32) * invf_ref[...]
    cos4 = jnp.cos(ang)
    sin4 = jnp.sin(ang)
    for k in range(4):
        def group_to(x, dst, k=k):
            shift = (32 * (dst - k)) % LANES
            return pltpu.roll(x, shift, 1) if shift else x
        cos_k = jnp.where(lane < 32, group_to(cos4, 0), group_to(cos4, 1))
        sin_k = jnp.where(lane < 96, -group_to(sin4, 2), group_to(sin4, 3))
        tab_ref[k * quarter:(k + 1) * quarter, :] = jnp.where(lane < 64, cos_k, sin_k)

    n = [_rmsnorm(x_ref[r, :], g_mix_ref[...]).astype(BF16) for r in rows]
    proj = [jnp.dot(ns, w_a_ref[...], preferred_element_type=F32) for ns in n]
    u = [jnp.dot(ns, w_u_ref[...], preferred_element_type=F32) for ns in n]
    g_q = g_q_ref[...] * q_scale
    cq = [_rmsnorm(pr[:, :Q_LORA], g_q).astype(BF16) for pr in proj]
    ckv = [_rmsnorm(pr[:, Q_LORA:kr_off], g_kv_ref[...]).astype(BF16) for pr in proj]

    def rope(pair, tab):
        t = pair * tab
        return (t + pltpu.roll(t, 64, 1))[:, :ROPE]

    for r, pr, cqs, ckvs in zip(rows, proj, cq, ckv):
        tab = tab_ref[r, :]
        q = jnp.dot(cqs, w_q_ref[...], preferred_element_type=F32)
        kn = jnp.dot(ckvs, w_k_ref[...], preferred_element_type=F32)
        vt_ref[:, r] = lax.dot_general(w_vt_ref[...], ckvs, (((1,), (1,)), ((), ())),
                                       preferred_element_type=F32).astype(BF16)
        kr = pr[:, kr_off:]
        kr2 = jnp.concatenate([kr, kr], axis=1)
        k_pe = rope(jnp.where(lane < 64, kr2, pltpu.roll(kr2, 32, 1)), tab).astype(BF16)
        for h in range(HEADS):
            q_ref[h, r, :NOPE] = q[:, h * NOPE:(h + 1) * NOPE].astype(BF16)
            pe = rope(q[:, HEADS * NOPE + h * LANES:HEADS * NOPE + (h + 1) * LANES], tab)
            q_ref[h, r, NOPE:] = pe.astype(BF16)
            k_ref[h, r, :NOPE] = kn[:, h * NOPE:(h + 1) * NOPE].astype(BF16)
            k_ref[h, r, NOPE:] = k_pe

    prev = halo_ref[...]
    halo_ref[...] = u[-1][sub - HALO:, :]
    for si, (r, us) in enumerate(zip(rows, u)):
        t_seq = (tile_in_seq * tm + si * sub) + lax.broadcasted_iota(jnp.int32, (sub, 1), 0)
        for g, w in enumerate(WINDOWS):
            cols = slice(g * POOL_CH, (g + 1) * POOL_CH)
            ug = us[:, cols]
            s = jnp.concatenate([prev[:, cols], ug], axis=0)
            for step in range(g + 1):
                s = s + pltpu.roll(s, 1 << step, 0)
            inv_cnt = 1.0 / jnp.minimum(t_seq + 1, w).astype(F32)
            p_ref[r, cols] = (s[HALO:, :] * inv_cnt - ug).astype(BF16)
        prev = us[sub - HALO:, :]


def _const_spec(shape):
    nd = len(shape)
    return pl.BlockSpec(shape, lambda *_: (0,) * nd, pipeline_mode=pl.Buffered(1))


def _front(x2, pos2, invf, g_mix, w_a, w_u, g_q, w_q, g_kv, w_k, w_vt, *, seq, tm, sub):
    t = x2.shape[0]
    tiles_per_seq = seq // tm
    q_scale = (QK ** -0.5) * math.log2(math.e)
    kern = functools.partial(_front_kernel, tm=tm, sub=sub, tiles_per_seq=tiles_per_seq,
                             q_scale=q_scale)
    return pl.pallas_call(
        kern,
        grid=(t // tm,),
        in_specs=[
            pl.BlockSpec((tm, D_MODEL), lambda i: (i, 0)),
            pl.BlockSpec((tm // 4, 4), lambda i: (i, 0)),
            _const_spec(invf.shape), _const_spec(g_mix.shape), _const_spec(w_a.shape),
            _const_spec(w_u.shape), _const_spec(g_q.shape), _const_spec(w_q.shape),
            _const_spec(g_kv.shape), _const_spec(w_k.shape), _const_spec(w_vt.shape),
        ],
        out_specs=[
            pl.BlockSpec((HEADS, tm, QK), lambda i: (0, i, 0)),
            pl.BlockSpec((HEADS, tm, QK), lambda i: (0, i, 0)),
            pl.BlockSpec((ATTN_W, tm), lambda i: (0, i)),
            pl.BlockSpec((tm, POOL_W), lambda i: (i, 0)),
        ],
        out_shape=[
            jax.ShapeDtypeStruct((HEADS, t, QK), BF16),
            jax.ShapeDtypeStruct((HEADS, t, QK), BF16),
            jax.ShapeDtypeStruct((ATTN_W, t), BF16),
            jax.ShapeDtypeStruct((t, POOL_W), BF16),
        ],
        scratch_shapes=[pltpu.VMEM((HALO, POOL_W), F32), pltpu.VMEM((tm, LANES), F32)],
        compiler_params=pltpu.CompilerParams(
            dimension_semantics=("arbitrary",), vmem_limit_bytes=FRONT_VMEM),
        name="front",
    )(x2, pos2, invf, g_mix, w_a, w_u, g_q, w_q, g_kv, w_k, w_vt)


def _attn_kernel(q_ref, k_ref, vt_ref, *rest, tq, tk, nq, ncast):
    o_ref, excess_ref = rest[ncast:ncast + 2]
    acc_ref = rest[-1]
    for src, dst in zip(rest[:ncast], rest[ncast + 2:-1]):
        dst[...] = src[...].astype(BF16)
    i = pl.program_id(1)
    half = tk // 2
    tri = (lax.broadcasted_iota(jnp.int32, (half, half), 1)
           >= lax.broadcasted_iota(jnp.int32, (half, half), 0))

    def scores(item):
        h, k0, rows, q0, cols, _ = item
        return lax.dot_general(k_ref[h, pl.ds(k0, rows), :], q_ref[h, q0:q0 + cols, :],
                               (((1,), (1,)), ((), ())), preferred_element_type=F32)

    def merge(full, part, q0):
        return part if q0 == 0 else jnp.concatenate([full[:, :q0], part], axis=1)

    def run(items):
        state = [None] * HEADS
        pending = {n: scores(items[n]) for n in range(min(LOOKAHEAD, len(items)))}
        for n, item in enumerate(items):
            if n + LOOKAHEAD < len(items):
                pending[n + LOOKAHEAD] = scores(items[n + LOOKAHEAD])
            h, k0, rows, q0, cols, masked = item
            assert q0 + cols == tq
            s = pending.pop(n)
            if masked:
                lower = jnp.where(tri, s[:, :half], -jnp.inf)
                s = lower if cols == half else jnp.concatenate([lower, s[:, half:]], axis=1)
            vt = vt_ref[h * VDIM:(h + 1) * VDIM, pl.ds(k0, rows)]
            smax = jnp.max(s, axis=0, keepdims=True)
            if state[h] is None:
                assert cols == tq
                p = jnp.exp2(s - smax)
                state[h] = (smax, jnp.sum(p, axis=0, keepdims=True), smax)
                acc_ref[h] = jnp.dot(vt, p.astype(BF16), preferred_element_type=F32)
                continue
            ref_all, l_all, top_all = state[h]
            ref, l, top = (a[:, q0:] for a in state[h])
            p = jnp.exp2(s - ref)
            l = l + jnp.sum(p, axis=0, keepdims=True)
            top = jnp.maximum(top, smax)
            acc_ref[h, :, q0:] += jnp.dot(vt, p.astype(BF16), preferred_element_type=F32)
            state[h] = (ref_all, merge(l_all, l, q0), merge(top_all, top, q0))
        return state

    def finish(state):
        for h in range(HEADS):
            o_ref[:, h * VDIM:(h + 1) * VDIM] = (acc_ref[h] / state[h][1]).T.astype(BF16)

    for qb in range(nq):
        @pl.when(i == qb)
        def _(qb=qb):
            items = ([(h, qb * tk, half, 0, tq, True) for h in range(HEADS)]
                     + [(h, qb * tk + half, half, half, tq - half, True) for h in range(HEADS)]
                     + [(h, j * tk, tk, 0, tq, False) for j in range(qb) for h in range(HEADS)])
            state = run(items)
            finish(state)
            excess = functools.reduce(jnp.maximum, [top - ref for ref, _, top in state])
            excess_ref[0] = functools.reduce(
                jnp.maximum, [excess[:, c:c + LANES] for c in range(0, tq, LANES)])


def _attn(q, k, vt, *, batch, seq, tq, cast=()):
    t = vt.shape[1]
    nq = seq // tq
    steps = batch * nq
    kern = functools.partial(_attn_kernel, tq=tq, tk=tq, nq=nq, ncast=len(cast))
    cast_specs = [pl.BlockSpec((w.shape[0] // steps, w.shape[1]), lambda b, i: (b * nq + i, 0))
                  for w in cast]
    return pl.pallas_call(
        kern,
        grid=(batch, nq),
        in_specs=[
            pl.BlockSpec((HEADS, tq, QK), lambda b, i: (0, b * nq + i, 0)),
            pl.BlockSpec((HEADS, seq, QK), lambda b, i: (0, b, 0)),
            pl.BlockSpec((ATTN_W, seq), lambda b, i: (0, b)),
        ] + cast_specs,
        out_specs=[pl.BlockSpec((tq, ATTN_W), lambda b, i: (b * nq + i, 0)),
                   pl.BlockSpec((1, 1, LANES), lambda b, i: (b * nq + i, 0, 0))] + cast_specs,
        out_shape=[jax.ShapeDtypeStruct((t, ATTN_W), BF16),
                   jax.ShapeDtypeStruct((steps, 1, LANES), F32)]
        + [jax.ShapeDtypeStruct(w.shape, BF16) for w in cast],
        scratch_shapes=[pltpu.VMEM((HEADS, VDIM, tq), F32)],
        compiler_params=pltpu.CompilerParams(
            dimension_semantics=("arbitrary", "arbitrary"), vmem_limit_bytes=ATTN_VMEM),
        name="attn",
    )(q, k, vt, *cast)


def _attn_redo_kernel(q_ref, k_ref, vt_ref, prev_ref, o_ref, m_ref, l_ref, acc_ref, *, tq, tk):
    del prev_ref
    i = pl.program_id(1)
    m_ref[...] = jnp.full_like(m_ref, -jnp.inf)
    l_ref[...] = jnp.zeros_like(l_ref)
    acc_ref[...] = jnp.zeros_like(acc_ref)
    qpos = i * tq + lax.broadcasted_iota(jnp.int32, (tk, tq), 1)

    def kv_block(j, carry):
        k0 = pl.multiple_of(j * tk, tk)
        visible = qpos >= k0 + lax.broadcasted_iota(jnp.int32, (tk, tq), 0)
        for h in range(HEADS):
            s = lax.dot_general(k_ref[h, pl.ds(k0, tk), :], q_ref[h],
                                (((1,), (1,)), ((), ())), preferred_element_type=F32)
            s = jnp.where(visible, s, -jnp.inf)
            m_old = m_ref[h]
            m = jnp.maximum(m_old, jnp.max(s, axis=0, keepdims=True))
            alpha = jnp.exp2(m_old - m)
            p = jnp.exp2(s - m)
            m_ref[h] = m
            l_ref[h] = alpha * l_ref[h] + jnp.sum(p, axis=0, keepdims=True)
            acc_ref[h] = alpha * acc_ref[h] + jnp.dot(
                vt_ref[h * VDIM:(h + 1) * VDIM, pl.ds(k0, tk)], p.astype(BF16),
                preferred_element_type=F32)
        return carry

    lax.fori_loop(0, i + 1, kv_block, 0)
    for h in range(HEADS):
        o_ref[:, h * VDIM:(h + 1) * VDIM] = (acc_ref[h] / l_ref[h]).T.astype(BF16)


def _attn_redo(q, k, vt, a, batches, *, seq, tq):
    nq = seq // tq
    return pl.pallas_call(
        functools.partial(_attn_redo_kernel, tq=tq, tk=tq),
        grid=(batches, nq),
        in_specs=[
            pl.BlockSpec((HEADS, tq, QK), lambda b, i: (0, b * nq + i, 0)),
            pl.BlockSpec((HEADS, seq, QK), lambda b, i: (0, b, 0)),
            pl.BlockSpec((ATTN_W, seq), lambda b, i: (0, b)),
            pl.BlockSpec(memory_space=pl.ANY),
        ],
        input_output_aliases={3: 0},
        out_specs=pl.BlockSpec((tq, ATTN_W), lambda b, i: (b * nq + i, 0)),
        out_shape=jax.ShapeDtypeStruct(a.shape, BF16),
        scratch_shapes=[pltpu.VMEM((HEADS, 1, tq), F32), pltpu.VMEM((HEADS, 1, tq), F32),
                        pltpu.VMEM((HEADS, VDIM, tq), F32)],
        compiler_params=pltpu.CompilerParams(
            dimension_semantics=("arbitrary", "arbitrary"), vmem_limit_bytes=ATTN_VMEM),
        name="attn_redo",
    )(q, k, vt, a)


def _back_kernel(x_ref, a_ref, p_ref, w_oa_ref, pool_w_ref, pool_scale_ref, w_op_ref, g_mlp_ref,
                 w_up_ref, w_down_ref, g_fin_ref, o_ref, w_pool_ref, *, tm, sub, fc):
    @pl.when(pl.program_id(0) == 0)
    def _():
        for g in range(GROUPS):
            ch = slice(g * POOL_CH, (g + 1) * POOL_CH)
            w_pool_ref[ch, :] = jnp.dot(
                pool_w_ref[g] * pool_scale_ref[:, ch], w_op_ref[ch, :],
                precision=lax.Precision.HIGHEST, preferred_element_type=F32).astype(BF16)

    rows = [pl.ds(r * sub, sub) for r in range(tm // sub)]
    h = [x_ref[r, :]
         + jnp.dot(a_ref[r, :], w_oa_ref[...], preferred_element_type=F32)
         + jnp.dot(p_ref[r, :], w_pool_ref[...], preferred_element_type=F32) for r in rows]
    n = [_rmsnorm(hs, g_mlp_ref[...]).astype(BF16) for hs in h]
    acc = [None] * len(rows)
    for c in range(D_FF // fc):
        for si, ns in enumerate(n):
            m = jnp.dot(ns, w_up_ref[:, c * fc:(c + 1) * fc], preferred_element_type=F32)
            r = jnp.square(jnp.maximum(m, 0.0)).astype(BF16)
            d = jnp.dot(r, w_down_ref[c * fc:(c + 1) * fc, :], preferred_element_type=F32)
            acc[si] = d if acc[si] is None else acc[si] + d
    for r, hs, accs in zip(rows, h, acc):
        o_ref[r, :] = _rmsnorm(hs + accs, g_fin_ref[...])


def _back(x2, a, p, w_oa, pool_w, pool_scale, w_op, g_mlp, w_up, w_down, g_fin, *, tm, sub, fc):
    t = x2.shape[0]
    kern = functools.partial(_back_kernel, tm=tm, sub=sub, fc=fc)
    return pl.pallas_call(
        kern,
        grid=(t // tm,),
        in_specs=[
            pl.BlockSpec((tm, D_MODEL), lambda i: (i, 0)),
            pl.BlockSpec((tm, ATTN_W), lambda i: (i, 0)),
            pl.BlockSpec((tm, POOL_W), lambda i: (i, 0)),
            _const_spec(w_oa.shape), _const_spec(pool_w.shape), _const_spec(pool_scale.shape),
            _const_spec(w_op.shape), _const_spec(g_mlp.shape),
            _const_spec(w_up.shape), _const_spec(w_down.shape), _const_spec(g_fin.shape),
        ],
        out_specs=pl.BlockSpec((tm, D_MODEL), lambda i: (i, 0)),
        out_shape=jax.ShapeDtypeStruct((t, D_MODEL), F32),
        scratch_shapes=[pltpu.VMEM((POOL_W, D_MODEL), BF16)],
        compiler_params=pltpu.CompilerParams(
            dimension_semantics=("arbitrary",), vmem_limit_bytes=BACK_VMEM),
        name="back",
    )(x2, a, p, w_oa, pool_w, pool_scale, w_op, g_mlp, w_up, w_down, g_fin)


def _half_swap(w):
    half = w.shape[-1] // 2
    return jnp.concatenate([w[..., half:], w[..., :half]], axis=-1)


def kernel(x, positions, norm_mix_g, w_in, q_norm_g, w_q_up, kv_norm_g, w_kv_up, pool_w,
           pool_scale, w_out, norm_mlp_g, w_mlp_up, w_mlp_down, norm_final_g):
    batch, seq, _ = x.shape
    t = batch * seq
    x2 = x.reshape(t, D_MODEL)
    pos2 = positions.reshape(t // FRONT_TM, 4, FRONT_TM // 4).transpose(0, 2, 1).reshape(t // 4, 4)
    inv_freq = 1.0 / (ROPE_THETA ** (jnp.arange(0, ROPE, 2, dtype=F32) / ROPE))
    invf = jnp.tile(inv_freq, 4).reshape(1, LANES)

    assert norm_mix_g.shape[0] == 1, "single-layer block only"
    u_col = Q_LORA + KV_LORA + ROPE
    w_a_l = w_in[0, :, :u_col].astype(BF16)
    w_u_l = w_in[0, :, u_col:].astype(BF16)
    wq = w_q_up[0].reshape(Q_LORA, HEADS, QK)
    wq_pe = wq[:, :, NOPE:]
    w_q_l = jnp.concatenate(
        [wq[:, :, :NOPE].reshape(Q_LORA, HEADS * NOPE),
         jnp.concatenate([wq_pe, _half_swap(wq_pe)], axis=-1).reshape(Q_LORA, HEADS * LANES)],
        axis=1).astype(BF16)
    wkv = w_kv_up[0].reshape(KV_LORA, HEADS, NOPE + VDIM)
    w_k_l = wkv[:, :, :NOPE].reshape(KV_LORA, HEADS * NOPE).astype(BF16)
    w_vt_l = wkv[:, :, NOPE:].reshape(KV_LORA, HEADS * VDIM).T.astype(BF16)

    q, k, vt, p = _front(
        x2, pos2, invf, norm_mix_g[0].reshape(1, D_MODEL), w_a_l, w_u_l,
        q_norm_g[0].reshape(1, Q_LORA), w_q_l, kv_norm_g[0].reshape(1, KV_LORA), w_k_l, w_vt_l,
        seq=seq, tm=FRONT_TM, sub=FRONT_SUB)
    a, excess, w_up_l, w_down_l = _attn(q, k, vt, batch=batch, seq=seq, tq=ATTN_TQ,
                                        cast=(w_mlp_up[0], w_mlp_down[0]))
    redo = jnp.where(jnp.max(excess) > EXP_HEADROOM, batch, 0).astype(jnp.int32)
    a = _attn_redo(q, k, vt, a, redo, seq=seq, tq=ATTN_TQ)
    out = _back(
        x2, a, p, w_out[0, :ATTN_W].astype(BF16), pool_w[0], pool_scale[0].reshape(1, POOL_W),
        w_out[0, ATTN_W:], norm_mlp_g[0].reshape(1, D_MODEL), w_up_l, w_down_l,
        norm_final_g.reshape(1, D_MODEL), tm=BACK_TM, sub=BACK_SUB, fc=BACK_FC)
    return out.reshape(batch, seq, D_MODEL)
```

```python
import functools
import math

import jax
import jax.numpy as jnp
from jax import lax
from jax.experimental import pallas as pl
from jax.experimental.pallas import tpu as pltpu

D_MODEL = 1024
HEADS = 4
NOPE = 128
ROPE = 64
VDIM = 128
QK = NOPE + ROPE
ATTN_W = HEADS * VDIM
Q_LORA = 384
KV_LORA = 256
GROUPS = 4
WINDOWS = (2, 4, 8, 16)
POOL_W = 512
POOL_CH = 128
D_FF = 4096
ROPE_THETA = 10000.0
EPS = 1e-6
HALO = 16
LANES = 128
LOOKAHEAD = 4
EXP_HEADROOM = 64.0
MIB = 1024 * 1024
FRONT_VMEM = 60 * MIB
ATTN_VMEM = 24 * MIB
BACK_VMEM = 56 * MIB
FRONT_TM = 2048
FRONT_SUB = 256
ATTN_TQ = 512
BACK_TM = 1024
BACK_SUB = 256
BACK_FC = 1024

F32 = jnp.float32
BF16 = jnp.bfloat16


def _rmsnorm(x, g):
    ms = jnp.mean(x * x, axis=-1, keepdims=True)
    return x * lax.rsqrt(ms + EPS) * g


def _front_kernel(x_ref, pos_ref, invf_ref, g_mix_ref, w_a_ref, w_u_ref, g_q_ref, w_q_ref, g_kv_ref,
                  w_k_ref, w_vt_ref,
                  q_ref, k_ref, vt_ref, p_ref, halo_ref, tab_ref, *, tm, sub, tiles_per_seq,
                  q_scale):
    i = pl.program_id(0)
    tile_in_seq = i % tiles_per_seq
    rows = [pl.ds(r * sub, sub) for r in range(tm // sub)]
    lane = lax.broadcasted_iota(jnp.int32, (1, LANES), 1)
    kr_off = Q_LORA + KV_LORA

    @pl.when(tile_in_seq == 0)
    def _():
        halo_ref[...] = jnp.zeros_like(halo_ref)

    quarter = tm // 4
    pos = pos_ref[...]
    pos4 = jnp.where(lane < 64, jnp.where(lane < 32, pos[:, 0:1], pos[:, 1:2]),
                     jnp.where(lane < 96, pos[:, 2:3], pos[:, 3:4]))
    ang = pos4.astype(F32) * invf_ref[...]
    cos4 = jnp.cos(ang)
    sin4 = jnp.sin(ang)
    for k in range(4):
        def group_to(x, dst, k=k):
            shift = (32 * (dst - k)) % LANES
            return pltpu.roll(x, shift, 1) if shift else x
        cos_k = jnp.where(lane < 32, group_to(cos4, 0), group_to(cos4, 1))
        sin_k = jnp.where(lane < 96, -group_to(sin4, 2), group_to(sin4, 3))
        tab_ref[k * quarter:(k + 1) * quarter, :] = jnp.where(lane < 64, cos_k, sin_k)

    n = [_rmsnorm(x_ref[r, :], g_mix_ref[...]).astype(BF16) for r in rows]
    proj = [jnp.dot(ns, w_a_ref[...], preferred_element_type=F32) for ns in n]
    u = [jnp.dot(ns, w_u_ref[...], preferred_element_type=F32) for ns in n]
    g_q = g_q_ref[...] * q_scale
    cq = [_rmsnorm(pr[:, :Q_LORA], g_q).astype(BF16) for pr in proj]
    ckv = [_rmsnorm(pr[:, Q_LORA:kr_off], g_kv_ref[...]).astype(BF16) for pr in proj]

    def rope(pair, tab):
        t = pair * tab
        return (t + pltpu.roll(t, 64, 1))[:, :ROPE]

    for r, pr, cqs, ckvs in zip(rows, proj, cq, ckv):
        tab = tab_ref[r, :]
        q = jnp.dot(cqs, w_q_ref[...], preferred_element_type=F32)
        kn = jnp.dot(ckvs, w_k_ref[...], preferred_element_type=F32)
        vt_ref[:, r] = lax.dot_general(w_vt_ref[...], ckvs, (((1,), (1,)), ((), ())),
                                       preferred_element_type=F32).astype(BF16)
        kr = pr[:, kr_off:]
        kr2 = jnp.concatenate([kr, kr], axis=1)
        k_pe = rope(jnp.where(lane < 64, kr2, pltpu.roll(kr2, 32, 1)), tab).astype(BF16)
        for h in range(HEADS):
            q_ref[h, r, :NOPE] = q[:, h * NOPE:(h + 1) * NOPE].astype(BF16)
            pe = rope(q[:, HEADS * NOPE + h * LANES:HEADS * NOPE + (h + 1) * LANES], tab)
            q_ref[h, r, NOPE:] = pe.astype(BF16)
            k_ref[h, r, :NOPE] = kn[:, h * NOPE:(h + 1) * NOPE].astype(BF16)
            k_ref[h, r, NOPE:] = k_pe

    prev = halo_ref[...]
    halo_ref[...] = u[-1][sub - HALO:, :]
    for si, (r, us) in enumerate(zip(rows, u)):
        t_seq = (tile_in_seq * tm + si * sub) + lax.broadcasted_iota(jnp.int32, (sub, 1), 0)
        for g, w in enumerate(WINDOWS):
            cols = slice(g * POOL_CH, (g + 1) * POOL_CH)
            ug = us[:, cols]
            s = jnp.concatenate([prev[:, cols], ug], axis=0)
            for step in range(g + 1):
                s = s + pltpu.roll(s, 1 << step, 0)
            inv_cnt = 1.0 / jnp.minimum(t_seq + 1, w).astype(F32)
            p_ref[r, cols] = (s[HALO:, :] * inv_cnt - ug).astype(BF16)
        prev = us[sub - HALO:, :]


def _const_spec(shape):
    nd = len(shape)
    return pl.BlockSpec(shape, lambda *_: (0,) * nd, pipeline_mode=pl.Buffered(1))


def _front(x2, pos2, invf, g_mix, w_a, w_u, g_q, w_q, g_kv, w_k, w_vt, *, seq, tm, sub):
    t = x2.shape[0]
    tiles_per_seq = seq // tm
    q_scale = (QK ** -0.5) * math.log2(math.e)
    kern = functools.partial(_front_kernel, tm=tm, sub=sub, tiles_per_seq=tiles_per_seq,
                             q_scale=q_scale)
    return pl.pallas_call(
        kern,
        grid=(t // tm,),
        in_specs=[
            pl.BlockSpec((tm, D_MODEL), lambda i: (i, 0)),
            pl.BlockSpec((tm // 4, 4), lambda i: (i, 0)),
            _const_spec(invf.shape), _const_spec(g_mix.shape), _const_spec(w_a.shape),
            _const_spec(w_u.shape), _const_spec(g_q.shape), _const_spec(w_q.shape),
            _const_spec(g_kv.shape), _const_spec(w_k.shape), _const_spec(w_vt.shape),
        ],
        out_specs=[
            pl.BlockSpec((HEADS, tm, QK), lambda i: (0, i, 0)),
            pl.BlockSpec((HEADS, tm, QK), lambda i: (0, i, 0)),
            pl.BlockSpec((ATTN_W, tm), lambda i: (0, i)),
            pl.BlockSpec((tm, POOL_W), lambda i: (i, 0)),
        ],
        out_shape=[
            jax.ShapeDtypeStruct((HEADS, t, QK), BF16),
            jax.ShapeDtypeStruct((HEADS, t, QK), BF16),
            jax.ShapeDtypeStruct((ATTN_W, t), BF16),
            jax.ShapeDtypeStruct((t, POOL_W), BF16),
        ],
        scratch_shapes=[pltpu.VMEM((HALO, POOL_W), F32), pltpu.VMEM((tm, LANES), F32)],
        compiler_params=pltpu.CompilerParams(
            dimension_semantics=("arbitrary",), vmem_limit_bytes=FRONT_VMEM),
        name="front",
    )(x2, pos2, invf, g_mix, w_a, w_u, g_q, w_q, g_kv, w_k, w_vt)


def _attn_kernel(q_ref, k_ref, vt_ref, *rest, tq, tk, nq, ncast):
    o_ref, excess_ref = rest[ncast:ncast + 2]
    acc_ref = rest[-1]
    for src, dst in zip(rest[:ncast], rest[ncast + 2:-1]):
        dst[...] = src[...].astype(BF16)
    i = pl.program_id(1)
    half = tk // 2
    tri = (lax.broadcasted_iota(jnp.int32, (half, half), 1)
           >= lax.broadcasted_iota(jnp.int32, (half, half), 0))

    def scores(item):
        h, k0, rows, q0, cols, _ = item
        return lax.dot_general(k_ref[h, pl.ds(k0, rows), :], q_ref[h, q0:q0 + cols, :],
                               (((1,), (1,)), ((), ())), preferred_element_type=F32)

    def merge(full, part, q0):
        return part if q0 == 0 else jnp.concatenate([full[:, :q0], part], axis=1)

    def run(items):
        state = [None] * HEADS
        pending = {n: scores(items[n]) for n in range(min(LOOKAHEAD, len(items)))}
        for n, item in enumerate(items):
            if n + LOOKAHEAD < len(items):
                pending[n + LOOKAHEAD] = scores(items[n + LOOKAHEAD])
            h, k0, rows, q0, cols, masked = item
            assert q0 + cols == tq
            s = pending.pop(n)
            if masked:
                lower = jnp.where(tri, s[:, :half], -jnp.inf)
                s = lower if cols == half else jnp.concatenate([lower, s[:, half:]], axis=1)
            vt = vt_ref[h * VDIM:(h + 1) * VDIM, pl.ds(k0, rows)]
            smax = jnp.max(s, axis=0, keepdims=True)
            if state[h] is None:
                assert cols == tq
                p = jnp.exp2(s - smax)
                state[h] = (smax, jnp.sum(p, axis=0, keepdims=True), smax)
                acc_ref[h] = jnp.dot(vt, p.astype(BF16), preferred_element_type=F32)
                continue
            ref_all, l_all, top_all = state[h]
            ref, l, top = (a[:, q0:] for a in state[h])
            p = jnp.exp2(s - ref)
            l = l + jnp.sum(p, axis=0, keepdims=True)
            top = jnp.maximum(top, smax)
            acc_ref[h, :, q0:] += jnp.dot(vt, p.astype(BF16), preferred_element_type=F32)
            state[h] = (ref_all, merge(l_all, l, q0), merge(top_all, top, q0))
        return state

    def finish(state):
        for h in range(HEADS):
            o_ref[:, h * VDIM:(h + 1) * VDIM] = (acc_ref[h] / state[h][1]).T.astype(BF16)

    for qb in range(nq):
        @pl.when(i == qb)
        def _(qb=qb):
            items = ([(h, qb * tk, half, 0, tq, True) for h in range(HEADS)]
                     + [(h, qb * tk + half, half, half, tq - half, True) for h in range(HEADS)]
                     + [(h, j * tk, tk, 0, tq, False) for j in range(qb) for h in range(HEADS)])
            state = run(items)
            finish(state)
            excess = functools.reduce(jnp.maximum, [top - ref for ref, _, top in state])
            excess_ref[0] = functools.reduce(
                jnp.maximum, [excess[:, c:c + LANES] for c in range(0, tq, LANES)])


def _attn(q, k, vt, *, batch, seq, tq, cast=()):
    t = vt.shape[1]
    nq = seq // tq
    steps = batch * nq
    kern = functools.partial(_attn_kernel, tq=tq, tk=tq, nq=nq, ncast=len(cast))
    cast_specs = [pl.BlockSpec((w.shape[0] // steps, w.shape[1]), lambda b, i: (b * nq + i, 0))
                  for w in cast]
    return pl.pallas_call(
        kern,
        grid=(batch, nq),
        in_specs=[
            pl.BlockSpec((HEADS, tq, QK), lambda b, i: (0, b * nq + i, 0)),
            pl.BlockSpec((HEADS, seq, QK), lambda b, i: (0, b, 0)),
            pl.BlockSpec((ATTN_W, seq), lambda b, i: (0, b)),
        ] + cast_specs,
        out_specs=[pl.BlockSpec((tq, ATTN_W), lambda b, i: (b * nq + i, 0)),
                   pl.BlockSpec((1, 1, LANES), lambda b, i: (b * nq + i, 0, 0))] + cast_specs,
        out_shape=[jax.ShapeDtypeStruct((t, ATTN_W), BF16),
                   jax.ShapeDtypeStruct((steps, 1, LANES), F32)]
        + [jax.ShapeDtypeStruct(w.shape, BF16) for w in cast],
        scratch_shapes=[pltpu.VMEM((HEADS, VDIM, tq), F32)],
        compiler_params=pltpu.CompilerParams(
            dimension_semantics=("arbitrary", "arbitrary"), vmem_limit_bytes=ATTN_VMEM),
        name="attn",
    )(q, k, vt, *cast)


def _attn_redo_kernel(q_ref, k_ref, vt_ref, prev_ref, o_ref, m_ref, l_ref, acc_ref, *, tq, tk):
    del prev_ref
    i = pl.program_id(1)
    m_ref[...] = jnp.full_like(m_ref, -jnp.inf)
    l_ref[...] = jnp.zeros_like(l_ref)
    acc_ref[...] = jnp.zeros_like(acc_ref)
    qpos = i * tq + lax.broadcasted_iota(jnp.int32, (tk, tq), 1)

    def kv_block(j, carry):
        k0 = pl.multiple_of(j * tk, tk)
        visible = qpos >= k0 + lax.broadcasted_iota(jnp.int32, (tk, tq), 0)
        for h in range(HEADS):
            s = lax.dot_general(k_ref[h, pl.ds(k0, tk), :], q_ref[h],
                                (((1,), (1,)), ((), ())), preferred_element_type=F32)
            s = jnp.where(visible, s, -jnp.inf)
            m_old = m_ref[h]
            m = jnp.maximum(m_old, jnp.max(s, axis=0, keepdims=True))
            alpha = jnp.exp2(m_old - m)
            p = jnp.exp2(s - m)
            m_ref[h] = m
            l_ref[h] = alpha * l_ref[h] + jnp.sum(p, axis=0, keepdims=True)
            acc_ref[h] = alpha * acc_ref[h] + jnp.dot(
                vt_ref[h * VDIM:(h + 1) * VDIM, pl.ds(k0, tk)], p.astype(BF16),
                preferred_element_type=F32)
        return carry

    lax.fori_loop(0, i + 1, kv_block, 0)
    for h in range(HEADS):
        o_ref[:, h * VDIM:(h + 1) * VDIM] = (acc_ref[h] / l_ref[h]).T.astype(BF16)


def _attn_redo(q, k, vt, a, batches, *, seq, tq):
    nq = seq // tq
    return pl.pallas_call(
        functools.partial(_attn_redo_kernel, tq=tq, tk=tq),
        grid=(batches, nq),
        in_specs=[
            pl.BlockSpec((HEADS, tq, QK), lambda b, i: (0, b * nq + i, 0)),
            pl.BlockSpec((HEADS, seq, QK), lambda b, i: (0, b, 0)),
            pl.BlockSpec((ATTN_W, seq), lambda b, i: (0, b)),
            pl.BlockSpec(memory_space=pl.ANY),
        ],
        input_output_aliases={3: 0},
        out_specs=pl.BlockSpec((tq, ATTN_W), lambda b, i: (b * nq + i, 0)),
        out_shape=jax.ShapeDtypeStruct(a.shape, BF16),
        scratch_shapes=[pltpu.VMEM((HEADS, 1, tq), F32), pltpu.VMEM((HEADS, 1, tq), F32),
                        pltpu.VMEM((HEADS, VDIM, tq), F32)],
        compiler_params=pltpu.CompilerParams(
            dimension_semantics=("arbitrary", "arbitrary"), vmem_limit_bytes=ATTN_VMEM),
        name="attn_redo",
    )(q, k, vt, a)


def _back_kernel(x_ref, a_ref, p_ref, w_oa_ref, pool_w_ref, pool_scale_ref, w_op_ref, g_mlp_ref,
                 w_up_ref, w_down_ref, g_fin_ref, o_ref, w_pool_ref, *, tm, sub, fc):
    @pl.when(pl.program_id(0) == 0)
    def _():
        for g in range(GROUPS):
            ch = slice(g * POOL_CH, (g + 1) * POOL_CH)
            w_pool_ref[ch, :] = jnp.dot(
                pool_w_ref[g] * pool_scale_ref[:, ch], w_op_ref[ch, :],
                precision=lax.Precision.HIGHEST, preferred_element_type=F32).astype(BF16)

    rows = [pl.ds(r * sub, sub) for r in range(tm // sub)]
    h = [x_ref[r, :]
         + jnp.dot(a_ref[r, :], w_oa_ref[...], preferred_element_type=F32)
         + jnp.dot(p_ref[r, :], w_pool_ref[...], preferred_element_type=F32) for r in rows]
    n = [_rmsnorm(hs, g_mlp_ref[...]).astype(BF16) for hs in h]
    acc = [None] * len(rows)
    for c in range(D_FF // fc):
        for si, ns in enumerate(n):
            m = jnp.dot(ns, w_up_ref[:, c * fc:(c + 1) * fc], preferred_element_type=F32)
            r = jnp.square(jnp.maximum(m, 0.0)).astype(BF16)
            d = jnp.dot(r, w_down_ref[c * fc:(c + 1) * fc, :], preferred_element_type=F32)
            acc[si] = d if acc[si] is None else acc[si] + d
    for r, hs, accs in zip(rows, h, acc):
        o_ref[r, :] = _rmsnorm(hs + accs, g_fin_ref[...])


def _back(x2, a, p, w_oa, pool_w, pool_scale, w_op, g_mlp, w_up, w_down, g_fin, *, tm, sub, fc):
    t = x2.shape[0]
    kern = functools.partial(_back_kernel, tm=tm, sub=sub, fc=fc)
    return pl.pallas_call(
        kern,
        grid=(t // tm,),
        in_specs=[
            pl.BlockSpec((tm, D_MODEL), lambda i: (i, 0)),
            pl.BlockSpec((tm, ATTN_W), lambda i: (i, 0)),
            pl.BlockSpec((tm, POOL_W), lambda i: (i, 0)),
            _const_spec(w_oa.shape), _const_spec(pool_w.shape), _const_spec(pool_scale.shape),
            _const_spec(w_op.shape), _const_spec(g_mlp.shape),
            _const_spec(w_up.shape), _const_spec(w_down.shape), _const_spec(g_fin.shape),
        ],
        out_specs=pl.BlockSpec((tm, D_MODEL), lambda i: (i, 0)),
        out_shape=jax.ShapeDtypeStruct((t, D_MODEL), F32),
        scratch_shapes=[pltpu.VMEM((POOL_W, D_MODEL), BF16)],
        compiler_params=pltpu.CompilerParams(
            dimension_semantics=("arbitrary",), vmem_limit_bytes=BACK_VMEM),
        name="back",
    )(x2, a, p, w_oa, pool_w, pool_scale, w_op, g_mlp, w_up, w_down, g_fin)


def _half_swap(w):
    half = w.shape[-1] // 2
    return jnp.concatenate([w[..., half:], w[..., :half]], axis=-1)


def kernel(x, positions, norm_mix_g, w_in, q_norm_g, w_q_up, kv_norm_g, w_kv_up, pool_w,
           pool_scale, w_out, norm_mlp_g, w_mlp_up, w_mlp_down, norm_final_g):
    batch, seq, _ = x.shape
    t = batch * seq
    x2 = x.reshape(t, D_MODEL)
    pos2 = positions.reshape(t // FRONT_TM, 4, FRONT_TM // 4).transpose(0, 2, 1).reshape(t // 4, 4)
    inv_freq = 1.0 / (ROPE_THETA ** (jnp.arange(0, ROPE, 2, dtype=F32) / ROPE))
    invf = jnp.tile(inv_freq, 4).reshape(1, LANES)

    assert norm_mix_g.shape[0] == 1, "single-layer block only"
    u_col = Q_LORA + KV_LORA + ROPE
    w_a_l = w_in[0, :, :u_col].astype(BF16)
    w_u_l = w_in[0, :, u_col:].astype(BF16)
    wq = w_q_up[0].reshape(Q_LORA, HEADS, QK)
    wq_pe = wq[:, :, NOPE:]
    w_q_l = jnp.concatenate(
        [wq[:, :, :NOPE].reshape(Q_LORA, HEADS * NOPE),
         jnp.concatenate([wq_pe, _half_swap(wq_pe)], axis=-1).reshape(Q_LORA, HEADS * LANES)],
        axis=1).astype(BF16)
    wkv = w_kv_up[0].reshape(KV_LORA, HEADS, NOPE + VDIM)
    w_k_l = wkv[:, :, :NOPE].reshape(KV_LORA, HEADS * NOPE).astype(BF16)
    w_vt_l = wkv[:, :, NOPE:].reshape(KV_LORA, HEADS * VDIM).T.astype(BF16)

    q, k, vt, p = _front(
        x2, pos2, invf, norm_mix_g[0].reshape(1, D_MODEL), w_a_l, w_u_l,
        q_norm_g[0].reshape(1, Q_LORA), w_q_l, kv_norm_g[0].reshape(1, KV_LORA), w_k_l, w_vt_l,
        seq=seq, tm=FRONT_TM, sub=FRONT_SUB)
    a, excess, w_up_l, w_down_l = _attn(q, k, vt, batch=batch, seq=seq, tq=ATTN_TQ,
                                        cast=(w_mlp_up[0], w_mlp_down[0]))
    redo = jnp.where(jnp.max(excess) > EXP_HEADROOM, batch, 0).astype(jnp.int32)
    a = _attn_redo(q, k, vt, a, redo, seq=seq, tq=ATTN_TQ)
    out = _back(
        x2, a, p, w_out[0, :ATTN_W].astype(BF16), pool_w[0], pool_scale[0].reshape(1, POOL_W),
        w_out[0, ATTN_W:], norm_mlp_g[0].reshape(1, D_MODEL), w_up_l, w_down_l,
        norm_final_g.reshape(1, D_MODEL), tm=BACK_TM, sub=BACK_SUB, fc=BACK_FC)
    return out.reshape(batch, seq, D_MODEL)
```

```python
import functools
import math

import jax
import jax.numpy as jnp
from jax import lax
from jax.experimental import pallas as pl
from jax.experimental.pallas import tpu as pltpu

D_MODEL = 1024
HEADS = 4
NOPE = 128
ROPE = 64
VDIM = 128
QK = NOPE + ROPE
ATTN_W = HEADS * VDIM
Q_LORA = 384
KV_LORA = 256
GROUPS = 4
WINDOWS = (2, 4, 8, 16)
POOL_W = 512
POOL_CH = 128
D_FF = 4096
ROPE_THETA = 10000.0
EPS = 1e-6
HALO = 16
LANES = 128
LOOKAHEAD = 4
EXP_HEADROOM = 64.0
MIB = 1024 * 1024
FRONT_VMEM = 60 * MIB
ATTN_VMEM = 32 * MIB
BACK_VMEM = 56 * MIB
FRONT_TM = 2048
FRONT_SUB = 256
ATTN_TQ = 512
ATTN_QPB = 2
BACK_TM = 1024
BACK_SUB = 256
BACK_FC = 1024

F32 = jnp.float32
BF16 = jnp.bfloat16


def _rmsnorm(x, g):
    ms = jnp.mean(x * x, axis=-1, keepdims=True)
    return x * lax.rsqrt(ms + EPS) * g


def _front_kernel(x_ref, pos_ref, invf_ref, g_mix_ref, w_a_ref, w_u_ref, g_q_ref, w_q_ref, g_kv_ref,
                  w_k_ref, w_vt_ref,
                  q_ref, k_ref, vt_ref, p_ref, halo_ref, tab_ref, *, tm, sub, tiles_per_seq,
                  q_scale):
    i = pl.program_id(0)
    tile_in_seq = i % tiles_per_seq
    rows = [pl.ds(r * sub, sub) for r in range(tm // sub)]
    lane = lax.broadcasted_iota(jnp.int32, (1, LANES), 1)
    kr_off = Q_LORA + KV_LORA

    @pl.when(tile_in_seq == 0)
    def _():
        halo_ref[...] = jnp.zeros_like(halo_ref)

    quarter = tm // 4
    pos = pos_ref[...]
    pos4 = jnp.where(lane < 64, jnp.where(lane < 32, pos[:, 0:1], pos[:, 1:2]),
                     jnp.where(lane < 96, pos[:, 2:3], pos[:, 3:4]))
    ang = pos4.astype(F32) * invf_ref[...]
    cos4 = jnp.cos(ang)
    sin4 = jnp.sin(ang)
    for k in range(4):
        def group_to(x, dst, k=k):
            shift = (32 * (dst - k)) % LANES
            return pltpu.roll(x, shift, 1) if shift else x
        cos_k = jnp.where(lane < 32, group_to(cos4, 0), group_to(cos4, 1))
        sin_k = jnp.where(lane < 96, -group_to(sin4, 2), group_to(sin4, 3))
        tab_ref[k * quarter:(k + 1) * quarter, :] = jnp.where(lane < 64, cos_k, sin_k)

    n = [_rmsnorm(x_ref[r, :], g_mix_ref[...]).astype(BF16) for r in rows]
    proj = [jnp.dot(ns, w_a_ref[...], preferred_element_type=F32) for ns in n]
    u = [jnp.dot(ns, w_u_ref[...], preferred_element_type=F32) for ns in n]
    g_q = g_q_ref[...] * q_scale
    cq = [_rmsnorm(pr[:, :Q_LORA], g_q).astype(BF16) for pr in proj]
    ckv = [_rmsnorm(pr[:, Q_LORA:kr_off], g_kv_ref[...]).astype(BF16) for pr in proj]

    def rope(pair, tab):
        t = pair * tab
        return (t + pltpu.roll(t, 64, 1))[:, :ROPE]

    for r, pr, cqs, ckvs in zip(rows, proj, cq, ckv):
        tab = tab_ref[r, :]
        q = jnp.dot(cqs, w_q_ref[...], preferred_element_type=F32)
        kn = jnp.dot(ckvs, w_k_ref[...], preferred_element_type=F32)
        vt_ref[:, r] = lax.dot_general(w_vt_ref[...], ckvs, (((1,), (1,)), ((), ())),
                                       preferred_element_type=F32).astype(BF16)
        kr = pr[:, kr_off:]
        kr2 = jnp.concatenate([kr, kr], axis=1)
        k_pe = rope(jnp.where(lane < 64, kr2, pltpu.roll(kr2, 32, 1)), tab).astype(BF16)
        for h in range(HEADS):
            q_ref[h, r, :NOPE] = q[:, h * NOPE:(h + 1) * NOPE].astype(BF16)
            pe = rope(q[:, HEADS * NOPE + h * LANES:HEADS * NOPE + (h + 1) * LANES], tab)
            q_ref[h, r, NOPE:] = pe.astype(BF16)
            k_ref[h, r, :NOPE] = kn[:, h * NOPE:(h + 1) * NOPE].astype(BF16)
            k_ref[h, r, NOPE:] = k_pe

    prev = halo_ref[...]
    halo_ref[...] = u[-1][sub - HALO:, :]
    for si, (r, us) in enumerate(zip(rows, u)):
        t_seq = (tile_in_seq * tm + si * sub) + lax.broadcasted_iota(jnp.int32, (sub, 1), 0)
        for g, w in enumerate(WINDOWS):
            cols = slice(g * POOL_CH, (g + 1) * POOL_CH)
            ug = us[:, cols]
            s = jnp.concatenate([prev[:, cols], ug], axis=0)
            for step in range(g + 1):
                s = s + pltpu.roll(s, 1 << step, 0)
            inv_cnt = 1.0 / jnp.minimum(t_seq + 1, w).astype(F32)
            p_ref[r, cols] = (s[HALO:, :] * inv_cnt - ug).astype(BF16)
        prev = us[sub - HALO:, :]


def _const_spec(shape):
    nd = len(shape)
    return pl.BlockSpec(shape, lambda *_: (0,) * nd, pipeline_mode=pl.Buffered(1))


def _front(x2, pos2, invf, g_mix, w_a, w_u, g_q, w_q, g_kv, w_k, w_vt, *, seq, tm, sub):
    t = x2.shape[0]
    tiles_per_seq = seq // tm
    q_scale = (QK ** -0.5) * math.log2(math.e)
    kern = functools.partial(_front_kernel, tm=tm, sub=sub, tiles_per_seq=tiles_per_seq,
                             q_scale=q_scale)
    return pl.pallas_call(
        kern,
        grid=(t // tm,),
        in_specs=[
            pl.BlockSpec((tm, D_MODEL), lambda i: (i, 0)),
            pl.BlockSpec((tm // 4, 4), lambda i: (i, 0)),
            _const_spec(invf.shape), _const_spec(g_mix.shape), _const_spec(w_a.shape),
            _const_spec(w_u.shape), _const_spec(g_q.shape), _const_spec(w_q.shape),
            _const_spec(g_kv.shape), _const_spec(w_k.shape), _const_spec(w_vt.shape),
        ],
        out_specs=[
            pl.BlockSpec((HEADS, tm, QK), lambda i: (0, i, 0)),
            pl.BlockSpec((HEADS, tm, QK), lambda i: (0, i, 0)),
            pl.BlockSpec((ATTN_W, tm), lambda i: (0, i)),
            pl.BlockSpec((tm, POOL_W), lambda i: (i, 0)),
        ],
        out_shape=[
            jax.ShapeDtypeStruct((HEADS, t, QK), BF16),
            jax.ShapeDtypeStruct((HEADS, t, QK), BF16),
            jax.ShapeDtypeStruct((ATTN_W, t), BF16),
            jax.ShapeDtypeStruct((t, POOL_W), BF16),
        ],
        scratch_shapes=[pltpu.VMEM((HALO, POOL_W), F32), pltpu.VMEM((tm, LANES), F32)],
        compiler_params=pltpu.CompilerParams(
            dimension_semantics=("arbitrary",), vmem_limit_bytes=FRONT_VMEM),
        name="front",
    )(x2, pos2, invf, g_mix, w_a, w_u, g_q, w_q, g_kv, w_k, w_vt)


def _attn_kernel(q_ref, k_ref, vt_ref, *rest, tq, tk, nq, qpb, ncast):
    o_ref, excess_ref = rest[ncast:ncast + 2]
    acc_ref = rest[-1]
    for src, dst in zip(rest[:ncast], rest[ncast + 2:-1]):
        dst[...] = src[...].astype(BF16)
    i = pl.program_id(1)
    half = tk // 2
    tri = (lax.broadcasted_iota(jnp.int32, (half, half), 1)
           >= lax.broadcasted_iota(jnp.int32, (half, half), 0))

    def scores(item):
        slot, k0, rows, q0, cols, _ = item
        lq, h = divmod(slot, HEADS)
        return lax.dot_general(k_ref[h, pl.ds(k0, rows), :],
                               q_ref[h, lq * tq + q0:lq * tq + q0 + cols, :],
                               (((1,), (1,)), ((), ())), preferred_element_type=F32)

    def merge(full, part, q0):
        return part if q0 == 0 else jnp.concatenate([full[:, :q0], part], axis=1)

    def run(items):
        state = [None] * (qpb * HEADS)
        pending = {n: scores(items[n]) for n in range(min(LOOKAHEAD, len(items)))}
        for n, item in enumerate(items):
            if n + LOOKAHEAD < len(items):
                pending[n + LOOKAHEAD] = scores(items[n + LOOKAHEAD])
            slot, k0, rows, q0, cols, masked = item
            h = slot % HEADS
            assert q0 + cols == tq
            s = pending.pop(n)
            if masked:
                lower = jnp.where(tri, s[:, :half], -jnp.inf)
                s = lower if cols == half else jnp.concatenate([lower, s[:, half:]], axis=1)
            vt = vt_ref[h * VDIM:(h + 1) * VDIM, pl.ds(k0, rows)]
            smax = jnp.max(s, axis=0, keepdims=True)
            if state[slot] is None:
                assert cols == tq
                p = jnp.exp2(s - smax)
                state[slot] = (smax, jnp.sum(p, axis=0, keepdims=True), smax)
                acc_ref[slot] = jnp.dot(vt, p.astype(BF16), preferred_element_type=F32)
                continue
            ref_all, l_all, top_all = state[slot]
            ref, l, top = (a[:, q0:] for a in state[slot])
            p = jnp.exp2(s - ref)
            l = l + jnp.sum(p, axis=0, keepdims=True)
            top = jnp.maximum(top, smax)
            acc_ref[slot, :, q0:] += jnp.dot(vt, p.astype(BF16), preferred_element_type=F32)
            state[slot] = (ref_all, merge(l_all, l, q0), merge(top_all, top, q0))
        return state

    def finish(state):
        for slot in range(qpb * HEADS):
            lq, h = divmod(slot, HEADS)
            o_ref[lq * tq:(lq + 1) * tq, h * VDIM:(h + 1) * VDIM] = (
                acc_ref[slot] / state[slot][1]).T.astype(BF16)

    def block_items(lq, qb):
        slots = range(lq * HEADS, (lq + 1) * HEADS)
        return ([(s, qb * tk, half, 0, tq, True) for s in slots]
                + [(s, qb * tk + half, half, half, tq - half, True) for s in slots]
                + [(s, j * tk, tk, 0, tq, False) for j in range(qb) for s in slots])

    for grp in range(nq // qpb):
        @pl.when(i == grp)
        def _(grp=grp):
            items = [it for lq in range(qpb) for it in block_items(lq, grp * qpb + lq)]
            state = run(items)
            finish(state)
            excess = functools.reduce(jnp.maximum, [top - ref for ref, _, top in state])
            excess_ref[0] = functools.reduce(
                jnp.maximum, [excess[:, c:c + LANES] for c in range(0, tq, LANES)])


def _attn(q, k, vt, *, batch, seq, tq, cast=()):
    t = vt.shape[1]
    nq = seq // tq
    ng = nq // ATTN_QPB
    steps = batch * ng
    kern = functools.partial(_attn_kernel, tq=tq, tk=tq, nq=nq, qpb=ATTN_QPB, ncast=len(cast))
    cast_specs = [pl.BlockSpec((w.shape[0] // steps, w.shape[1]), lambda b, i: (b * ng + i, 0))
                  for w in cast]
    return pl.pallas_call(
        kern,
        grid=(batch, ng),
        in_specs=[
            pl.BlockSpec((HEADS, ATTN_QPB * tq, QK), lambda b, i: (0, b * ng + i, 0)),
            pl.BlockSpec((HEADS, seq, QK), lambda b, i: (0, b, 0)),
            pl.BlockSpec((ATTN_W, seq), lambda b, i: (0, b)),
        ] + cast_specs,
        out_specs=[pl.BlockSpec((ATTN_QPB * tq, ATTN_W), lambda b, i: (b * ng + i, 0)),
                   pl.BlockSpec((1, 1, LANES), lambda b, i: (b * ng + i, 0, 0))] + cast_specs,
        out_shape=[jax.ShapeDtypeStruct((t, ATTN_W), BF16),
                   jax.ShapeDtypeStruct((steps, 1, LANES), F32)]
        + [jax.ShapeDtypeStruct(w.shape, BF16) for w in cast],
        scratch_shapes=[pltpu.VMEM((ATTN_QPB * HEADS, VDIM, tq), F32)],
        compiler_params=pltpu.CompilerParams(
            dimension_semantics=("arbitrary", "arbitrary"), vmem_limit_bytes=ATTN_VMEM),
        name="attn",
    )(q, k, vt, *cast)


def _attn_redo_kernel(q_ref, k_ref, vt_ref, prev_ref, o_ref, m_ref, l_ref, acc_ref, *, tq, tk):
    del prev_ref
    i = pl.program_id(1)
    m_ref[...] = jnp.full_like(m_ref, -jnp.inf)
    l_ref[...] = jnp.zeros_like(l_ref)
    acc_ref[...] = jnp.zeros_like(acc_ref)
    qpos = i * tq + lax.broadcasted_iota(jnp.int32, (tk, tq), 1)

    def kv_block(j, carry):
        k0 = pl.multiple_of(j * tk, tk)
        visible = qpos >= k0 + lax.broadcasted_iota(jnp.int32, (tk, tq), 0)
        for h in range(HEADS):
            s = lax.dot_general(k_ref[h, pl.ds(k0, tk), :], q_ref[h],
                                (((1,), (1,)), ((), ())), preferred_element_type=F32)
            s = jnp.where(visible, s, -jnp.inf)
            m_old = m_ref[h]
            m = jnp.maximum(m_old, jnp.max(s, axis=0, keepdims=True))
            alpha = jnp.exp2(m_old - m)
            p = jnp.exp2(s - m)
            m_ref[h] = m
            l_ref[h] = alpha * l_ref[h] + jnp.sum(p, axis=0, keepdims=True)
            acc_ref[h] = alpha * acc_ref[h] + jnp.dot(
                vt_ref[h * VDIM:(h + 1) * VDIM, pl.ds(k0, tk)], p.astype(BF16),
                preferred_element_type=F32)
        return carry

    lax.fori_loop(0, i + 1, kv_block, 0)
    for h in range(HEADS):
        o_ref[:, h * VDIM:(h + 1) * VDIM] = (acc_ref[h] / l_ref[h]).T.astype(BF16)


def _attn_redo(q, k, vt, a, batches, *, seq, tq):
    nq = seq // tq
    return pl.pallas_call(
        functools.partial(_attn_redo_kernel, tq=tq, tk=tq),
        grid=(batches, nq),
        in_specs=[
            pl.BlockSpec((HEADS, tq, QK), lambda b, i: (0, b * nq + i, 0)),
            pl.BlockSpec((HEADS, seq, QK), lambda b, i: (0, b, 0)),
            pl.BlockSpec((ATTN_W, seq), lambda b, i: (0, b)),
            pl.BlockSpec(memory_space=pl.ANY),
        ],
        input_output_aliases={3: 0},
        out_specs=pl.BlockSpec((tq, ATTN_W), lambda b, i: (b * nq + i, 0)),
        out_shape=jax.ShapeDtypeStruct(a.shape, BF16),
        scratch_shapes=[pltpu.VMEM((HEADS, 1, tq), F32), pltpu.VMEM((HEADS, 1, tq), F32),
                        pltpu.VMEM((HEADS, VDIM, tq), F32)],
        compiler_params=pltpu.CompilerParams(
            dimension_semantics=("arbitrary", "arbitrary"), vmem_limit_bytes=ATTN_VMEM),
        name="attn_redo",
    )(q, k, vt, a)


def _back_kernel(x_ref, a_ref, p_ref, w_oa_ref, pool_w_ref, pool_scale_ref, w_op_ref, g_mlp_ref,
                 w_up_hbm, w_down_hbm, g_fin_ref, o_ref, w_pool_ref, w_up_ref, w_down_ref, w_sem,
                 *, tm, sub, fc):
    @pl.when(pl.program_id(0) == 0)
    def _():
        copies = (pltpu.make_async_copy(w_up_hbm, w_up_ref, w_sem.at[0]),
                  pltpu.make_async_copy(w_down_hbm, w_down_ref, w_sem.at[1]))
        for copy in copies:
            copy.start()
        for g in range(GROUPS):
            ch = slice(g * POOL_CH, (g + 1) * POOL_CH)
            w_pool_ref[ch, :] = jnp.dot(
                pool_w_ref[g] * pool_scale_ref[:, ch], w_op_ref[ch, :],
                precision=lax.Precision.HIGHEST, preferred_element_type=F32).astype(BF16)
        for copy in copies:
            copy.wait()

    rows = [pl.ds(r * sub, sub) for r in range(tm // sub)]
    h = [x_ref[r, :]
         + jnp.dot(a_ref[r, :], w_oa_ref[...], preferred_element_type=F32)
         + jnp.dot(p_ref[r, :], w_pool_ref[...], preferred_element_type=F32) for r in rows]
    n = [_rmsnorm(hs, g_mlp_ref[...]).astype(BF16) for hs in h]
    acc = [None] * len(rows)
    for c in range(D_FF // fc):
        for si, ns in enumerate(n):
            m = jnp.dot(ns, w_up_ref[:, c * fc:(c + 1) * fc], preferred_element_type=F32)
            r = jnp.square(jnp.maximum(m, 0.0)).astype(BF16)
            d = jnp.dot(r, w_down_ref[c * fc:(c + 1) * fc, :], preferred_element_type=F32)
            acc[si] = d if acc[si] is None else acc[si] + d
    for r, hs, accs in zip(rows, h, acc):
        o_ref[r, :] = _rmsnorm(hs + accs, g_fin_ref[...])


def _back(x2, a, p, w_oa, pool_w, pool_scale, w_op, g_mlp, w_up, w_down, g_fin, *, tm, sub, fc):
    t = x2.shape[0]
    kern = functools.partial(_back_kernel, tm=tm, sub=sub, fc=fc)
    return pl.pallas_call(
        kern,
        grid=(t // tm,),
        in_specs=[
            pl.BlockSpec((tm, D_MODEL), lambda i: (i, 0)),
            pl.BlockSpec((tm, ATTN_W), lambda i: (i, 0)),
            pl.BlockSpec((tm, POOL_W), lambda i: (i, 0)),
            _const_spec(w_oa.shape), _const_spec(pool_w.shape), _const_spec(pool_scale.shape),
            _const_spec(w_op.shape), _const_spec(g_mlp.shape),
            pl.BlockSpec(memory_space=pl.ANY), pl.BlockSpec(memory_space=pl.ANY),
            _const_spec(g_fin.shape),
        ],
        out_specs=pl.BlockSpec((tm, D_MODEL), lambda i: (i, 0)),
        out_shape=jax.ShapeDtypeStruct((t, D_MODEL), F32),
        scratch_shapes=[pltpu.VMEM((POOL_W, D_MODEL), BF16), pltpu.VMEM(w_up.shape, BF16),
                        pltpu.VMEM(w_down.shape, BF16), pltpu.SemaphoreType.DMA((2,))],
        compiler_params=pltpu.CompilerParams(
            dimension_semantics=("arbitrary",), vmem_limit_bytes=BACK_VMEM),
        name="back",
    )(x2, a, p, w_oa, pool_w, pool_scale, w_op, g_mlp, w_up, w_down, g_fin)


def _half_swap(w):
    half = w.shape[-1] // 2
    return jnp.concatenate([w[..., half:], w[..., :half]], axis=-1)


def kernel(x, positions, norm_mix_g, w_in, q_norm_g, w_q_up, kv_norm_g, w_kv_up, pool_w,
           pool_scale, w_out, norm_mlp_g, w_mlp_up, w_mlp_down, norm_final_g):
    batch, seq, _ = x.shape
    t = batch * seq
    x2 = x.reshape(t, D_MODEL)
    pos2 = positions.reshape(t // FRONT_TM, 4, FRONT_TM // 4).transpose(0, 2, 1).reshape(t // 4, 4)
    inv_freq = 1.0 / (ROPE_THETA ** (jnp.arange(0, ROPE, 2, dtype=F32) / ROPE))
    invf = jnp.tile(inv_freq, 4).reshape(1, LANES)

    assert norm_mix_g.shape[0] == 1, "single-layer block only"
    u_col = Q_LORA + KV_LORA + ROPE
    w_a_l = w_in[0, :, :u_col].astype(BF16)
    w_u_l = w_in[0, :, u_col:].astype(BF16)
    wq = w_q_up[0].reshape(Q_LORA, HEADS, QK)
    wq_pe = wq[:, :, NOPE:]
    w_q_l = jnp.concatenate(
        [wq[:, :, :NOPE].reshape(Q_LORA, HEADS * NOPE),
         jnp.concatenate([wq_pe, _half_swap(wq_pe)], axis=-1).reshape(Q_LORA, HEADS * LANES)],
        axis=1).astype(BF16)
    wkv = w_kv_up[0].reshape(KV_LORA, HEADS, NOPE + VDIM)
    w_k_l = wkv[:, :, :NOPE].reshape(KV_LORA, HEADS * NOPE).astype(BF16)
    w_vt_l = wkv[:, :, NOPE:].reshape(KV_LORA, HEADS * VDIM).T.astype(BF16)

    q, k, vt, p = _front(
        x2, pos2, invf, norm_mix_g[0].reshape(1, D_MODEL), w_a_l, w_u_l,
        q_norm_g[0].reshape(1, Q_LORA), w_q_l, kv_norm_g[0].reshape(1, KV_LORA), w_k_l, w_vt_l,
        seq=seq, tm=FRONT_TM, sub=FRONT_SUB)
    a, excess, w_up_l, w_down_l = _attn(q, k, vt, batch=batch, seq=seq, tq=ATTN_TQ,
                                        cast=(w_mlp_up[0], w_mlp_down[0]))
    redo = jnp.where(jnp.max(excess) > EXP_HEADROOM, batch, 0).astype(jnp.int32)
    a = _attn_redo(q, k, vt, a, redo, seq=seq, tq=ATTN_TQ)
    out = _back(
        x2, a, p, w_out[0, :ATTN_W].astype(BF16), pool_w[0], pool_scale[0].reshape(1, POOL_W),
        w_out[0, ATTN_W:], norm_mlp_g[0].reshape(1, D_MODEL), w_up_l, w_down_l,
        norm_final_g.reshape(1, D_MODEL), tm=BACK_TM, sub=BACK_SUB, fc=BACK_FC)
    return out.reshape(batch, seq, D_MODEL)
```
